```python
import jax
import jax.numpy as jnp
from jax import lax
import numpy as np

D_MODEL = 2048
BATCH = 4
SEQ = 4096
DEPTH = 2

C_MIX = D_MODEL // 2
HEAD_DIM = 64
H_RWKV = C_MIX // HEAD_DIM
D_DECAY_LORA = 64
D_AAA_LORA = 64
D_MV_LORA = 32
D_GATE_LORA = 160
RWKV_GN_EPS = 64e-5
H_LRU = C_MIX // HEAD_DIM
LRU_CONV = 4
LRU_C = 8.0
CONF_CONV = 31
N_EXPERTS = 32
TOP_K = 4
D_FF_EXPERT = D_MODEL // 2
SWIGLU_ALPHA = 1.702
SWIGLU_LIMIT = 7.0
MOE_BLOCK = 512
LN_EPS = 1e-5
DEEPNORM_ALPHA = (2 * DEPTH) ** 0.25
DEEPNORM_BETA = (8 * DEPTH) ** -0.25

RWKV_SPLITS = [C_MIX, C_MIX, C_MIX, D_DECAY_LORA, D_AAA_LORA, D_GATE_LORA]
N_RWKV = sum(RWKV_SPLITS)
IN_SPLITS = [N_RWKV, C_MIX, C_MIX, 2 * C_MIX, 3 * D_MODEL]
N_IN = sum(IN_SPLITS)

kernel_name = 'hybrid_rwkv7_rglru_conformer_moe_deepnorm'


def _split(t, sizes):
    return jnp.split(t, np.cumsum(sizes)[:-1].tolist(), axis=-1)


def layer_norm(x, g, b, eps=LN_EPS):
    xf = x.astype(jnp.float32)
    mu = jnp.mean(xf, -1, keepdims=True)
    var = jnp.mean(jnp.square(xf - mu), -1, keepdims=True)
    return ((xf - mu) * lax.rsqrt(var + eps) * g + b).astype(x.dtype)


def token_shift(p):
    return jnp.pad(p, ((0, 0), (1, 0), (0, 0)))[:, :-1]


def causal_depthwise_conv(x, w, b):
    width, ch = w.shape
    y = lax.conv_general_dilated(x, w[:, None, :], (1,), [(width - 1, 0)],
                                 dimension_numbers=('NWC', 'WIO', 'NWC'),
                                 feature_group_count=ch)
    return y + b


def _rwkv7_step(state, inp):
    r, w, k, v, a, b = inp
    sa = jnp.einsum('bhij,bhj->bhi', state, a)
    state = state * w[:, :, None, :] + sa[..., None] * b[:, :, None, :] + v[..., None] * k[:, :, None, :]
    return state, jnp.einsum('bhij,bhj->bhi', state, r)


def rwkv7_mixer(p, v_first, p_vres, mu, mu_vres, w0, w2, a0, a2, v0, v2, g2, k_k, k_a, r_k, gn_g, gn_b):
    dt = p.dtype
    B, S, _ = p.shape
    f32 = jnp.float32
    p = p + (token_shift(p) - p) * mu
    r, k, v, wl, al, gl = _split(p, RWKV_SPLITS)
    w_log = -jax.nn.softplus(-(w0 + jnp.tanh(wl) @ w2)) - 0.5
    a = jax.nn.sigmoid(a0 + al @ a2)
    g = jax.nn.sigmoid(gl) @ g2
    if v_first is None:
        v_first = v
    else:
        p_vres = p_vres + (token_shift(p_vres) - p_vres) * mu_vres
        v = v + (v_first - v) * jax.nn.sigmoid(v0 + p_vres @ v2)
    heads = lambda t: t.astype(f32).reshape(B, S, H_RWKV, HEAD_DIM)
    kk = heads(k * k_k)
    kk = kk / jnp.maximum(jnp.linalg.norm(kk, axis=-1, keepdims=True), 1e-12)
    k = heads(k * (1 + (a - 1) * k_a))
    r, v, a = heads(r), heads(v), heads(a)
    decay = jnp.exp(-jnp.exp(heads(w_log)))
    to_seq = lambda t: jnp.moveaxis(t, 1, 0)
    state0 = jnp.zeros((B, H_RWKV, HEAD_DIM, HEAD_DIM), f32)
    _, y = lax.scan(_rwkv7_step, state0, tuple(to_seq(t) for t in (r, decay, k, v, -kk, kk * a)))
    y = jnp.moveaxis(y, 0, 1)
    mu_y = jnp.mean(y, -1, keepdims=True)
    var_y = jnp.mean(jnp.square(y - mu_y), -1, keepdims=True)
    y = (y - mu_y) * lax.rsqrt(var_y + RWKV_GN_EPS) * gn_g.reshape(H_RWKV, HEAD_DIM) + gn_b.reshape(H_RWKV, HEAD_DIM)
    y = y + jnp.sum(r * k * r_k, -1, keepdims=True) * v
    return y.reshape(B, S, C_MIX).astype(dt) * g, v_first


def _linear_combine(e1, e2):
    a1, b1 = e1
    a2, b2 = e2
    return a1 * a2, a2 * b1 + b2


def rglru_mixer(gate_in, x_in, conv_w, conv_b, wa, ba, wx, bx, lam):
    B, S, _ = x_in.shape
    xc = causal_depthwise_conv(x_in, conv_w, conv_b)
    xh = xc.reshape(B, S, H_LRU, HEAD_DIM)
    r = jax.nn.sigmoid(jnp.einsum('bshi,hij->bshj', xh, wa).reshape(B, S, C_MIX) + ba)
    i = jax.nn.sigmoid(jnp.einsum('bshi,hij->bshj', xh, wx).reshape(B, S, C_MIX) + bx)
    log_a = (-LRU_C * r * jax.nn.softplus(-lam)).astype(jnp.float32)
    a = jnp.exp(log_a)
    b = jnp.sqrt(-jnp.expm1(2.0 * log_a)) * (i * xc).astype(jnp.float32)
    _, h = lax.associative_scan(_linear_combine, (a, b), axis=1)
    return h.astype(x_in.dtype) * jax.nn.gelu(gate_in)


def conformer_conv_module(u, conv_w, conv_b, ln_g, ln_b):
    val, gate = jnp.split(u, 2, axis=-1)
    c = causal_depthwise_conv(val * jax.nn.sigmoid(gate), conv_w, conv_b)
    return jax.nn.silu(layer_norm(c, ln_g, ln_b))


def moe_ffn(x, router_w, router_b, w_gu, b_gu, w_down, b_down):
    B, S, D = x.shape
    T = B * S
    TK = T * TOP_K
    n_blocks = -(-TK // MOE_BLOCK) + N_EXPERTS
    xf = x.reshape(T, D)
    logits = (xf @ router_w + router_b).astype(jnp.float32)
    top_val, top_idx = lax.top_k(logits, TOP_K)
    gates = jax.nn.softmax(top_val, axis=-1).astype(x.dtype)
    flat_e = top_idx.reshape(TK)
    flat_t = jnp.repeat(jnp.arange(T, dtype=jnp.int32), TOP_K)
    order = jnp.argsort(flat_e)
    sorted_e = flat_e[order]
    counts = jnp.bincount(flat_e, length=N_EXPERTS)
    padded = (counts + MOE_BLOCK - 1) // MOE_BLOCK * MOE_BLOCK
    pad_end = jnp.cumsum(padded)
    pad_start = pad_end - padded
    start = jnp.cumsum(counts) - counts
    dest = pad_start[sorted_e] + jnp.arange(TK) - start[sorted_e]
    row_tok = jnp.zeros((n_blocks * MOE_BLOCK,), jnp.int32).at[dest].set(flat_t[order])
    row_w = jnp.zeros((n_blocks * MOE_BLOCK,), x.dtype).at[dest].set(gates.reshape(TK)[order])
    block_e = jnp.minimum(jnp.searchsorted(pad_end, jnp.arange(n_blocks) * MOE_BLOCK, side='right'), N_EXPERTS - 1)

    def expert_block(args):
        rows, e = args
        gu = xf[rows] @ w_gu[e] + b_gu[e]
        gate, up = jnp.split(gu, 2, axis=-1)
        gate = jnp.minimum(gate, SWIGLU_LIMIT)
        up = jnp.clip(up, -SWIGLU_LIMIT, SWIGLU_LIMIT)
        h = (up + 1.0) * gate * jax.nn.sigmoid(SWIGLU_ALPHA * gate)
        return h @ w_down[e] + b_down[e]

    out = lax.map(expert_block, (row_tok.reshape(n_blocks, MOE_BLOCK), block_e))
    out = out.reshape(-1, D) * row_w[:, None]
    return jnp.zeros_like(xf).at[row_tok].add(out).reshape(B, S, D)


def setup_inputs(seed: int = 0) -> dict:
    key = jax.random.key(seed)
    ks = iter(jax.random.split(key, 48))
    f32 = jnp.float32
    L, Lr, D, C, N = DEPTH, DEPTH - 1, D_MODEL, C_MIX, HEAD_DIM

    def nrm(shape, scale):
        return jax.random.normal(next(ks), shape, f32) * scale

    def unif(shape, lo, hi):
        return jax.random.uniform(next(ks), shape, f32, lo, hi)

    u = unif((L, C), 0.9, 0.999)
    s = u ** (1.0 / LRU_C)
    lam = jnp.log(s) - jnp.log1p(-s)
    return {
        'x': nrm((BATCH, SEQ, D), 1.0),
        'w_in': nrm((L, D, N_IN), D ** -0.5),
        'w_in_vres': nrm((Lr, D, D_MV_LORA), D ** -0.5),
        'shift_mu': unif((L, N_RWKV), 0.0, 1.0),
        'shift_mu_vres': unif((Lr, D_MV_LORA), 0.0, 1.0),
        'rwkv_w0': unif((L, C), -6.5, -1.5),
        'rwkv_w2': nrm((L, D_DECAY_LORA, C), 0.5 * D_DECAY_LORA ** -0.5),
        'rwkv_a0': nrm((L, C), 0.5),
        'rwkv_a2': nrm((L, D_AAA_LORA, C), 0.5 * D_AAA_LORA ** -0.5),
        'rwkv_v0': 1.0 + nrm((Lr, C), 0.1),
        'rwkv_v2': nrm((Lr, D_MV_LORA, C), 0.5 * D_MV_LORA ** -0.5),
        'rwkv_g2': nrm((L, D_GATE_LORA, C), D_GATE_LORA ** -0.5),
        'rwkv_k_k': 0.85 + nrm((L, C), 0.05),
        'rwkv_k_a': 1.0 + nrm((L, C), 0.05),
        'rwkv_r_k': nrm((L, H_RWKV, N), 0.1),
        'rwkv_gn_g': 1.0 + nrm((L, C), 0.02),
        'rwkv_gn_b': nrm((L, C), 0.02),
        'lru_conv_w': nrm((L, LRU_CONV, C), LRU_CONV ** -0.5),
        'lru_conv_b': nrm((L, C), 0.01),
        'lru_wa': nrm((L, H_LRU, N, N), N ** -0.5),
        'lru_ba': nrm((L, C), 0.01),
        'lru_wx': nrm((L, H_LRU, N, N), N ** -0.5),
        'lru_bx': nrm((L, C), 0.01),
        'lru_lambda': lam,
        'conf_conv_w': nrm((L, CONF_CONV, C), CONF_CONV ** -0.5),
        'conf_conv_b': nrm((L, C), 0.01),
        'conf_ln_g': 1.0 + nrm((L, C), 0.02),
        'conf_ln_b': nrm((L, C), 0.02),
        'w_branch': nrm((L, 3, C, D), C ** -0.5),
        'w_out': nrm((L, D, D), DEEPNORM_BETA * D ** -0.5),
        'ln1_g': 1.0 + nrm((L, D), 0.02),
        'ln1_b': nrm((L, D), 0.02),
        'router_w': nrm((L, D, N_EXPERTS), D ** -0.5),
        'router_b': nrm((L, N_EXPERTS), 0.01),
        'exp_w_gu': nrm((L, N_EXPERTS, D, 2 * D_FF_EXPERT), D ** -0.5),
        'exp_b_gu': nrm((L, N_EXPERTS, 2 * D_FF_EXPERT), 0.01),
        'exp_w_down': nrm((L, N_EXPERTS, D_FF_EXPERT, D), DEEPNORM_BETA * D_FF_EXPERT ** -0.5),
        'exp_b_down': nrm((L, N_EXPERTS, D), 0.01),
        'ln2_g': 1.0 + nrm((L, D), 0.02),
        'ln2_b': nrm((L, D), 0.02),
    }


def reference(x, w_in, w_in_vres, shift_mu, shift_mu_vres, rwkv_w0, rwkv_w2, rwkv_a0, rwkv_a2,
              rwkv_v0, rwkv_v2, rwkv_g2, rwkv_k_k, rwkv_k_a, rwkv_r_k, rwkv_gn_g, rwkv_gn_b,
              lru_conv_w, lru_conv_b, lru_wa, lru_ba, lru_wx, lru_bx, lru_lambda,
              conf_conv_w, conf_conv_b, conf_ln_g, conf_ln_b, w_branch, w_out, ln1_g, ln1_b,
              router_w, router_b, exp_w_gu, exp_b_gu, exp_w_down, exp_b_down, ln2_g, ln2_b):
    v_first = None
    for l in range(DEPTH):
        if l == 0:
            proj = x @ w_in[l]
            p_vres, mu_vres, v0, v2 = None, None, None, None
        else:
            proj_all = x @ jnp.concatenate([w_in[l], w_in_vres[l - 1]], axis=1)
            proj, p_vres = proj_all[..., :N_IN], proj_all[..., N_IN:]
            mu_vres, v0, v2 = shift_mu_vres[l - 1], rwkv_v0[l - 1], rwkv_v2[l - 1]
        p_rwkv, lru_gate, lru_x, conf_u, merge = _split(proj, IN_SPLITS)
        y_a, v_first = rwkv7_mixer(p_rwkv, v_first, p_vres, shift_mu[l], mu_vres,
                                   rwkv_w0[l], rwkv_w2[l], rwkv_a0[l], rwkv_a2[l], v0, v2,
                                   rwkv_g2[l], rwkv_k_k[l], rwkv_k_a[l], rwkv_r_k[l],
                                   rwkv_gn_g[l], rwkv_gn_b[l])
        y_b = rglru_mixer(lru_gate, lru_x, lru_conv_w[l], lru_conv_b[l], lru_wa[l], lru_ba[l],
                          lru_wx[l], lru_bx[l], lru_lambda[l])
        y_c = conformer_conv_module(conf_u, conf_conv_w[l], conf_conv_b[l], conf_ln_g[l], conf_ln_b[l])
        g_a, g_b, g_c = jnp.split(jax.nn.sigmoid(merge), 3, axis=-1)
        mixed = (g_a * (y_a @ w_branch[l, 0]) + g_b * (y_b @ w_branch[l, 1])
                 + g_c * (y_c @ w_branch[l, 2]))
        x = layer_norm(DEEPNORM_ALPHA * x + mixed @ w_out[l], ln1_g[l], ln1_b[l])
        ffn = moe_ffn(x, router_w[l], router_b[l], exp_w_gu[l], exp_b_gu[l], exp_w_down[l], exp_b_down[l])
        x = layer_norm(DEEPNORM_ALPHA * x + ffn, ln2_g[l], ln2_b[l])
    return x
```

```python
import functools

import jax
import jax.numpy as jnp
from jax import lax
from jax.experimental import pallas as pl
from jax.experimental.pallas import tpu as pltpu

F32 = jnp.float32
BF16 = jnp.bfloat16

HEAD_DIM = 64
RWKV_CHUNK = 64
RWKV_GN_EPS = 64e-5
LRU_C = 8.0
LN_EPS = 1e-5
TOP_K = 4
MOE_BLOCK = 512
SWIGLU_ALPHA = 1.702
SWIGLU_LIMIT = 7.0
LANE = 128
VMEM_LIMIT = 56 * 1024 * 1024


def _cparams(*sem):
    return pltpu.CompilerParams(dimension_semantics=sem, vmem_limit_bytes=VMEM_LIMIT)


def _dot(a, b):
    return jnp.dot(a, b, preferred_element_type=F32)


def _dot_nt(a, b):
    return lax.dot_general(a, b, (((1,), (1,)), ((), ())), preferred_element_type=F32)


def _dot_tn(a, b):
    return lax.dot_general(a, b, (((0,), (0,)), ((), ())), preferred_element_type=F32)


def _split_hi_lo(w):
    hi = w.astype(BF16)
    lo = (w - hi.astype(F32)).astype(BF16)
    return hi, lo


def _dot3(a, b_hi, b_lo):
    a_hi = a.astype(BF16)
    a_lo = (a - a_hi.astype(F32)).astype(BF16)
    return _dot(a_hi, b_hi) + (_dot(a_lo, b_hi) + _dot(a_hi, b_lo))


def _softplus(z):
    return jnp.maximum(z, 0.0) + jnp.log(1.0 + jnp.exp(-jnp.abs(z)))


def _layer_norm(v, g, b):
    mu = jnp.mean(v, -1, keepdims=True)
    c = v - mu
    var = jnp.mean(c * c, -1, keepdims=True)
    return c * lax.rsqrt(var + LN_EPS) * g + b


def _mm_kernel(x_ref, w_ref, o_ref, *, act):
    acc = _dot(x_ref[...], w_ref[...])
    if act == "sigmoid":
        acc = jax.nn.sigmoid(acc)
    o_ref[...] = acc.astype(o_ref.dtype)


def _matmul(x, w, out_dtype, tm, tn, act=None):
    m, k = x.shape
    n = w.shape[1]
    return pl.pallas_call(
        functools.partial(_mm_kernel, act=act),
        grid=(n // tn, m // tm),
        in_specs=[pl.BlockSpec((tm, k), lambda j, i: (i, 0)),
                  pl.BlockSpec((k, tn), lambda j, i: (0, j))],
        out_specs=pl.BlockSpec((tm, tn), lambda j, i: (i, j)),
        out_shape=jax.ShapeDtypeStruct((m, n), out_dtype),
        compiler_params=_cparams("parallel", "parallel"),
        name="in_proj",
    )(x, w)


RWKV_COLS = 3584
_C_LORA = 3072
_C_GATE = 3200
_C_VRES = 3456


def _rwkv_prep_kernel(*refs, tiles_per_batch, has_vres, c_mix):
    if has_vres:
        (p_ref, pprev_ref, mu_ref, w0_ref, a0_ref, w2h, w2l, a2h, a2l, g2_ref,
         v0_ref, v2h, v2l, vfirst_ref, r_o, k_o, v_o, lw_o, alr_o, g_o) = refs
    else:
        (p_ref, pprev_ref, mu_ref, w0_ref, a0_ref, w2h, w2l, a2h, a2l, g2_ref,
         r_o, k_o, v_o, lw_o, alr_o, g_o) = refs
    i = pl.program_id(0)
    tm = p_ref.shape[0]
    first = (i % tiles_per_batch) == 0

    def shifted(lo, hi):
        p = p_ref[:, lo:hi]
        prev_row = jnp.where(first, 0.0, pprev_ref[7:8, lo:hi])
        rows = lax.broadcasted_iota(jnp.int32, p.shape, 0)
        prev = jnp.where(rows == 0, prev_row, pltpu.roll(p, 1, axis=0))
        return p + (prev - p) * mu_ref[:, lo:hi]

    def put(o_ref, val):
        for h in range(c_mix // HEAD_DIM):
            o_ref[0, h] = val[:, h * HEAD_DIM:(h + 1) * HEAD_DIM].astype(o_ref.dtype)

    put(r_o, shifted(0, c_mix))
    put(k_o, shifted(c_mix, 2 * c_mix))
    lora = shifted(_C_LORA, _C_LORA + LANE)
    w_log = -_softplus(-(w0_ref[...] + _dot3(jnp.tanh(lora), w2h[...], w2l[...]))) - 0.5
    put(lw_o, -jnp.exp(w_log))
    put(alr_o, jax.nn.sigmoid(a0_ref[...] + _dot3(lora, a2h[...], a2l[...])))
    gate_in = jax.nn.sigmoid(shifted(_C_GATE, _C_GATE + 2 * LANE))
    put(g_o, _dot(gate_in.astype(BF16), g2_ref[...]))
    v = shifted(2 * c_mix, 3 * c_mix)
    if has_vres:
        pv = shifted(_C_VRES, _C_VRES + LANE)
        mix = jax.nn.sigmoid(v0_ref[...] + _dot3(pv, v2h[...], v2l[...]))
        v = v + (vfirst_ref[...] - v) * mix
    v_o[...] = v


def _rwkv_prep(p, mu, w0, a0, w2, a2, g2, vres, batch, seq, tm=256):
    t = p.shape[0]
    c_mix = w0.shape[-1]
    n_heads = c_mix // HEAD_DIM
    has_vres = vres is not None
    tiles_per_batch = seq // tm
    row = lambda i: (i, 0)
    const = lambda i: (0, 0)
    hm = lambda i: (i // tiles_per_batch, 0, i % tiles_per_batch, 0)
    in_specs = [
        pl.BlockSpec((tm, RWKV_COLS), row),
        pl.BlockSpec((8, RWKV_COLS), lambda i: (jnp.maximum(i * (tm // 8) - 1, 0), 0)),
        pl.BlockSpec((1, RWKV_COLS), const),
        pl.BlockSpec((1, c_mix), const), pl.BlockSpec((1, c_mix), const),
        pl.BlockSpec((LANE, c_mix), const), pl.BlockSpec((LANE, c_mix), const),
        pl.BlockSpec((LANE, c_mix), const), pl.BlockSpec((LANE, c_mix), const),
        pl.BlockSpec((2 * LANE, c_mix), const),
    ]
    args = [p, p, mu, w0, a0, *_split_hi_lo(w2), *_split_hi_lo(a2), g2.astype(BF16)]
    if has_vres:
        v0, v2, v_first = vres
        in_specs += [pl.BlockSpec((1, c_mix), const),
                     pl.BlockSpec((LANE, c_mix), const), pl.BlockSpec((LANE, c_mix), const),
                     pl.BlockSpec((tm, c_mix), row)]
        args += [v0, *_split_hi_lo(v2), v_first]
    hm_shape = jax.ShapeDtypeStruct((batch, n_heads, seq, HEAD_DIM), F32)
    hm_spec = pl.BlockSpec((1, n_heads, tm, HEAD_DIM), hm)
    out_shape = [hm_shape, hm_shape, jax.ShapeDtypeStruct((t, c_mix), F32), hm_shape, hm_shape, hm_shape]
    out_specs = [hm_spec, hm_spec, pl.BlockSpec((tm, c_mix), row), hm_spec, hm_spec, hm_spec]
    return pl.pallas_call(
        functools.partial(_rwkv_prep_kernel, tiles_per_batch=tiles_per_batch,
                          has_vres=has_vres, c_mix=c_mix),
        grid=(t // tm,), in_specs=in_specs, out_specs=out_specs, out_shape=out_shape,
        compiler_params=_cparams("parallel"), name="rwkv_prep",
    )(*args)


def _rwkv_chunk_kernel(r_ref, k_ref, v_ref, lw_ref, alr_ref, g_ref,
                       kk_ref, ka_ref, rk_ref, gng_ref, gnb_ref, y_ref, st_ref, *, heads):
    c = pl.program_id(2)
    L = RWKV_CHUNK
    n = HEAD_DIM

    @pl.when(c == 0)
    def _():
        st_ref[...] = jnp.zeros_like(st_ref)

    ti = lax.broadcasted_iota(jnp.int32, (L, L), 0)
    si = lax.broadcasted_iota(jnp.int32, (L, L), 1)
    strict = ti > si
    incl = ti >= si
    tril = jnp.where(incl, 1.0, 0.0).astype(BF16)
    eye = (lax.broadcasted_iota(jnp.int32, (n, n), 0) == lax.broadcasted_iota(jnp.int32, (n, n), 1))
    eye_l = jnp.where(ti == si, 1.0, 0.0)
    bf = lambda t: t.astype(BF16)

    def head(h, carry):
        r = r_ref[0, h]
        k_raw = k_ref[0, h]
        v = v_ref[0, h]
        lw = lw_ref[0, h]
        alr = alr_ref[0, h]
        kk = k_raw * kk_ref[h]
        kk = kk / jnp.maximum(jnp.sqrt(jnp.sum(kk * kk, -1, keepdims=True)), 1e-12)
        k = k_raw * (1.0 + (alr - 1.0) * ka_ref[h])
        a = -kk
        b = kk * alr
        l0 = bf(lw)
        rem = lw - l0.astype(F32)
        l1 = bf(rem)
        l2 = bf(rem - l1.astype(F32))
        cum = _dot(tril, l0) + (_dot(tril, l1) + _dot(tril, l2))
        cl = cum[L - 1:L, :]
        e_neg = jnp.exp(-cum)
        e_tail = jnp.exp(cl - cum)
        at = bf(a * jnp.exp(cum - lw))
        bt = bf(b * e_neg)
        kt = bf(k * e_neg)
        rt = r * jnp.exp(cum)
        bp = bf(b * e_tail)
        kp = bf(k * e_tail)
        vb = bf(v)
        a_ab = jnp.where(strict, _dot_nt(at, bt), 0.0)
        a_ak = jnp.where(strict, _dot_nt(at, kt), 0.0)
        m_rb = bf(jnp.where(incl, _dot_nt(bf(rt), bt), 0.0))
        m_rk = bf(jnp.where(incl, _dot_nt(bf(rt), kt), 0.0))
        pw = a_ab
        tinv = eye_l + a_ab
        for _ in range(5):
            pwb = bf(pw)
            pw = _dot(pwb, pwb)
            tinv = tinv + _dot(bf(tinv), bf(pw))
        tb = bf(tinv)
        e1 = bf(_dot(tb, bf(_dot(bf(a_ak), vb))))
        ah = bf(_dot(tb, at))
        rh = rt + _dot(m_rb, ah)
        o1 = _dot(m_rb, e1) + _dot(m_rk, vb)
        phi = jnp.where(eye, jnp.exp(cl), 0.0) + _dot_tn(bp, ah)
        psi = _dot_tn(bp, e1) + _dot_tn(kp, vb)
        st = st_ref[h]
        stb = bf(st)
        y = _dot(bf(rh), stb) + o1
        st_ref[h] = _dot(bf(phi), stb) + psi
        mu = jnp.mean(y, -1, keepdims=True)
        yc = y - mu
        var = jnp.mean(yc * yc, -1, keepdims=True)
        yn = yc * lax.rsqrt(var + RWKV_GN_EPS) * gng_ref[h] + gnb_ref[h]
        yn = yn + jnp.sum(r * k * rk_ref[h], -1, keepdims=True) * v
        y_ref[0, h] = (yn * g_ref[0, h]).astype(y_ref.dtype)
        return carry

    lax.fori_loop(0, heads, head, 0)


def _rwkv_chunk(r, k, v, lw, alr, g, k_k, k_a, r_k, gn_g, gn_b, heads_per_step=8):
    b, h, s, n = r.shape
    hb = heads_per_step
    seq_spec = pl.BlockSpec((1, hb, RWKV_CHUNK, n), lambda bi, hi, ci: (bi, hi, ci, 0))
    par_spec = pl.BlockSpec((hb, 1, n), lambda bi, hi, ci: (hi, 0, 0))
    par = lambda w: w.reshape(h, 1, n)
    return pl.pallas_call(
        functools.partial(_rwkv_chunk_kernel, heads=hb),
        grid=(b, h // hb, s // RWKV_CHUNK),
        in_specs=[seq_spec] * 6 + [par_spec] * 5,
        out_specs=seq_spec,
        out_shape=jax.ShapeDtypeStruct((b, h, s, n), BF16),
        scratch_shapes=[pltpu.VMEM((hb, n, n), F32)],
        compiler_params=_cparams("parallel", "parallel", "arbitrary"),
        name="rwkv_chunk",
    )(r, k, v, lw, alr, g, par(k_k), par(k_a), par(r_k), par(gn_g), par(gn_b))


def _shift_rows(x, d, fill):
    rows = lax.broadcasted_iota(jnp.int32, x.shape, 0)
    return jnp.where(rows < d, fill, pltpu.roll(x, d, axis=0))


def _lru_kernel(gate_ref, x_ref, cw_ref, cb_ref, wa_ref, ba_ref, wx_ref, bx_ref, lam_ref,
                y_ref, ext_ref, h_ref, *, conv_w):
    ti = pl.program_id(1)
    tm = x_ref.shape[0]

    @pl.when(ti == 0)
    def _():
        ext_ref[0:8, :] = jnp.zeros((8, ext_ref.shape[1]), F32)
        h_ref[...] = jnp.zeros_like(h_ref)

    x = x_ref[...]
    ext_ref[8:8 + tm, :] = x
    xc = cb_ref[...] + cw_ref[conv_w - 1:conv_w, :] * x
    for d in range(1, conv_w):
        xc = xc + cw_ref[conv_w - 1 - d:conv_w - d, :] * ext_ref[8 - d:8 - d + tm, :]
    ext_ref[0:8, :] = x[tm - 8:tm, :]
    xb = xc.astype(BF16)
    rg = jax.nn.sigmoid(_dot(xb, wa_ref[...]) + ba_ref[...])
    ig = jax.nn.sigmoid(_dot(xb, wx_ref[...]) + bx_ref[...])
    log_a = -LRU_C * rg * _softplus(-lam_ref[...])
    a = jnp.exp(log_a)
    th = jnp.tanh(log_a)
    b = jnp.sqrt(-2.0 * th / (1.0 - th)) * (ig * xc)
    d = 1
    while d < tm:
        b = b + a * _shift_rows(b, d, 0.0)
        a = a * _shift_rows(a, d, 1.0)
        d *= 2
    h = a * h_ref[...] + b
    h_ref[...] = h[tm - 1:tm, :]
    y_ref[...] = (h * jax.nn.gelu(gate_ref[...], approximate=True)).astype(y_ref.dtype)


def _block_diag(w):
    h, n, _ = w.shape
    eye = jnp.eye(h, dtype=w.dtype)
    return (w[:, :, None, :] * eye[:, None, :, None]).reshape(h * n, h * n)


def _lru(proj, conv_w, conv_b, wa, ba, wx, bx, lam, batch, seq, tm=256):
    c = conv_b.shape[-1]
    tiles = seq // tm
    row0 = lambda b, i: (b * tiles + i, 0)
    row1 = lambda b, i: (b * tiles + i, 1)
    const = lambda b, i: (0, 0)
    vec = pl.BlockSpec((1, c), const)
    width = conv_w.shape[0]
    return pl.pallas_call(
        functools.partial(_lru_kernel, conv_w=width),
        grid=(batch, tiles),
        in_specs=[pl.BlockSpec((tm, c), row0), pl.BlockSpec((tm, c), row1),
                  pl.BlockSpec((width, c), const), vec,
                  pl.BlockSpec((c, c), const), vec, pl.BlockSpec((c, c), const), vec, vec],
        out_specs=pl.BlockSpec((tm, c), row0),
        out_shape=jax.ShapeDtypeStruct((batch * seq, c), BF16),
        scratch_shapes=[pltpu.VMEM((tm + 8, c), F32), pltpu.VMEM((1, c), F32)],
        compiler_params=_cparams("parallel", "arbitrary"),
        name="rglru",
    )(proj, proj, conv_w, conv_b, _block_diag(wa).astype(BF16), ba, _block_diag(wx).astype(BF16), bx, lam)


_CONF_HALO = 32


def _conf_kernel(val_ref, gate_ref, cw_ref, cb_ref, g_ref, b_ref, y_ref, ext_ref, *, conv_w):
    ti = pl.program_id(1)
    tm = val_ref.shape[0]

    @pl.when(ti == 0)
    def _():
        ext_ref[0:_CONF_HALO, :] = jnp.zeros((_CONF_HALO, ext_ref.shape[1]), F32)

    u = val_ref[...] * jax.nn.sigmoid(gate_ref[...])
    ext_ref[_CONF_HALO:_CONF_HALO + tm, :] = u
    acc = cb_ref[...] + cw_ref[conv_w - 1:conv_w, :] * u
    for d in range(1, conv_w):
        acc = acc + cw_ref[conv_w - 1 - d:conv_w - d, :] * ext_ref[_CONF_HALO - d:_CONF_HALO - d + tm, :]
    ext_ref[0:_CONF_HALO, :] = u[tm - _CONF_HALO:tm, :]
    y = _layer_norm(acc, g_ref[...], b_ref[...])
    y_ref[...] = (y * jax.nn.sigmoid(y)).astype(y_ref.dtype)


def _conformer(proj, conv_w, conv_b, ln_g, ln_b, batch, seq, tm=256):
    c = conv_b.shape[-1]
    tiles = seq // tm
    row0 = lambda b, i: (b * tiles + i, 0)
    row1 = lambda b, i: (b * tiles + i, 1)
    const = lambda b, i: (0, 0)
    vec = pl.BlockSpec((1, c), const)
    width = conv_w.shape[0]
    return pl.pallas_call(
        functools.partial(_conf_kernel, conv_w=width),
        grid=(batch, tiles),
        in_specs=[pl.BlockSpec((tm, c), row0), pl.BlockSpec((tm, c), row1),
                  pl.BlockSpec((width, c), const), vec, vec, vec],
        out_specs=pl.BlockSpec((tm, c), row0),
        out_shape=jax.ShapeDtypeStruct((batch * seq, c), BF16),
        scratch_shapes=[pltpu.VMEM((tm + _CONF_HALO, c), F32)],
        compiler_params=_cparams("parallel", "arbitrary"),
        name="conformer",
    )(proj, proj, conv_w, conv_b, ln_g, ln_b)


def _merge_kernel(ya_ref, yb_ref, yc_ref, wb_ref, ga_ref, gb_ref, gc_ref, o_ref):
    acc = ga_ref[...].astype(F32) * _dot(ya_ref[...], wb_ref[0])
    acc = acc + gb_ref[...].astype(F32) * _dot(yb_ref[...], wb_ref[1])
    acc = acc + gc_ref[...].astype(F32) * _dot(yc_ref[...], wb_ref[2])
    o_ref[...] = acc.astype(o_ref.dtype)


def _merge(ya, yb, yc, w_branch, gates, tm=1024, tn=1024):
    t, c = ya.shape
    d = w_branch.shape[-1]
    nj = d // tn
    ysp = pl.BlockSpec((tm, c), lambda j, i: (i, 0))
    gsp = lambda k: pl.BlockSpec((tm, tn), lambda j, i: (i, k * nj + j))
    return pl.pallas_call(
        _merge_kernel,
        grid=(nj, t // tm),
        in_specs=[ysp, ysp, ysp, pl.BlockSpec((3, c, tn), lambda j, i: (0, 0, j)),
                  gsp(0), gsp(1), gsp(2)],
        out_specs=pl.BlockSpec((tm, tn), lambda j, i: (i, j)),
        out_shape=jax.ShapeDtypeStruct((t, d), BF16),
        compiler_params=_cparams("parallel", "parallel"),
        name="merge",
    )(ya, yb, yc, w_branch, gates, gates, gates)


def _out_ln_kernel(m_ref, x_ref, w_ref, g_ref, b_ref, rwh_ref, rwl_ref, rb_ref,
                   o_ref, ob_ref, lg_ref, *, alpha):
    y = alpha * x_ref[...] + _dot(m_ref[...], w_ref[...])
    y = _layer_norm(y, g_ref[...], b_ref[...])
    o_ref[...] = y
    ob_ref[...] = y.astype(BF16)
    lg_ref[...] = _dot3(y, rwh_ref[...], rwl_ref[...]) + rb_ref[...]


def _out_ln(mixed, x, w_out, ln_g, ln_b, router_w, router_b, alpha, tm=512):
    t, d = x.shape
    row = lambda i: (i, 0)
    const = lambda i: (0, 0)
    ne = router_w.shape[-1]
    rw = jnp.pad(router_w, ((0, 0), (0, LANE - ne)))
    rb = jnp.pad(router_b, ((0, 0), (0, LANE - ne)), constant_values=-1e30)
    return pl.pallas_call(
        functools.partial(_out_ln_kernel, alpha=alpha),
        grid=(t // tm,),
        in_specs=[pl.BlockSpec((tm, d), row), pl.BlockSpec((tm, d), row), pl.BlockSpec((d, d), const),
                  pl.BlockSpec((1, d), const), pl.BlockSpec((1, d), const),
                  pl.BlockSpec((d, LANE), const), pl.BlockSpec((d, LANE), const),
                  pl.BlockSpec((1, LANE), const)],
        out_specs=[pl.BlockSpec((tm, d), row), pl.BlockSpec((tm, d), row), pl.BlockSpec((tm, LANE), row)],
        out_shape=[jax.ShapeDtypeStruct((t, d), F32), jax.ShapeDtypeStruct((t, d), BF16),
                   jax.ShapeDtypeStruct((t, LANE), F32)],
        compiler_params=_cparams("parallel"),
        name="out_ln",
    )(mixed, x, w_out, ln_g, ln_b, *_split_hi_lo(rw), rb)


def _expert_kernel(be_ref, nused_ref, x_ref, wgu_ref, bgu_ref, wd_ref, bd_ref, o_ref, *, d_ff):
    i = pl.program_id(0)

    @pl.when(i < nused_ref[0])
    def _():
        gu = _dot(x_ref[...], wgu_ref[0]) + bgu_ref[0]
        gate = jnp.minimum(gu[:, :d_ff], SWIGLU_LIMIT)
        up = jnp.clip(gu[:, d_ff:], -SWIGLU_LIMIT, SWIGLU_LIMIT)
        h = (up + 1.0) * gate * jax.nn.sigmoid(SWIGLU_ALPHA * gate)
        o_ref[...] = (_dot(h.astype(BF16), wd_ref[0]) + bd_ref[0]).astype(o_ref.dtype)

    @pl.when(i >= nused_ref[0])
    def _():
        o_ref[...] = jnp.zeros_like(o_ref)


def _experts(xs, block_e, n_used, w_gu, b_gu, w_down, b_down):
    rows, d = xs.shape
    n_blocks = rows // MOE_BLOCK
    ne, d_ff, _ = w_down.shape
    xrow = lambda i, be, nu: (jnp.minimum(i, nu[0] - 1), 0)
    grid_spec = pltpu.PrefetchScalarGridSpec(
        num_scalar_prefetch=2,
        grid=(n_blocks,),
        in_specs=[pl.BlockSpec((MOE_BLOCK, d), xrow),
                  pl.BlockSpec((1, d, 2 * d_ff), lambda i, be, nu: (be[i], 0, 0)),
                  pl.BlockSpec((1, 1, 2 * d_ff), lambda i, be, nu: (be[i], 0, 0)),
                  pl.BlockSpec((1, d_ff, d), lambda i, be, nu: (be[i], 0, 0)),
                  pl.BlockSpec((1, 1, d), lambda i, be, nu: (be[i], 0, 0))],
        out_specs=pl.BlockSpec((MOE_BLOCK, d), lambda i, be, nu: (i, 0)),
    )
    return pl.pallas_call(
        functools.partial(_expert_kernel, d_ff=d_ff),
        grid_spec=grid_spec,
        out_shape=jax.ShapeDtypeStruct((rows, d), BF16),
        compiler_params=_cparams("arbitrary"),
        name="experts",
    )(block_e, n_used, xs, w_gu, b_gu.reshape(ne, 1, 2 * d_ff), w_down, b_down.reshape(ne, 1, d))


def _combine_ln_kernel(yg_ref, gw_ref, x_ref, g_ref, b_ref, o_ref, *, alpha, d):
    acc = alpha * x_ref[...]
    for k in range(TOP_K):
        acc = acc + gw_ref[:, k:k + 1] * yg_ref[:, k * d:(k + 1) * d].astype(F32)
    o_ref[...] = _layer_norm(acc, g_ref[...], b_ref[...])


def _combine_ln(yg, gw, x, ln_g, ln_b, alpha, tm=512):
    t, d = x.shape
    row = lambda i: (i, 0)
    const = lambda i: (0, 0)
    return pl.pallas_call(
        functools.partial(_combine_ln_kernel, alpha=alpha, d=d),
        grid=(t // tm,),
        in_specs=[pl.BlockSpec((tm, TOP_K * d), row), pl.BlockSpec((tm, TOP_K), row),
                  pl.BlockSpec((tm, d), row), pl.BlockSpec((1, d), const), pl.BlockSpec((1, d), const)],
        out_specs=pl.BlockSpec((tm, d), row),
        out_shape=jax.ShapeDtypeStruct((t, d), F32),
        compiler_params=_cparams("parallel"),
        name="combine_ln",
    )(yg, gw, x, ln_g, ln_b)


def _route(logits, n_experts):
    t = logits.shape[0]
    tk = t * TOP_K
    n_blocks = -(-tk // MOE_BLOCK) + n_experts
    top_val, top_idx = lax.top_k(logits, TOP_K)
    gates = jax.nn.softmax(top_val, axis=-1)
    flat_e = top_idx.reshape(tk)
    order = jnp.argsort(flat_e)
    sorted_e = flat_e[order]
    counts = jnp.bincount(flat_e, length=n_experts)
    padded = (counts + MOE_BLOCK - 1) // MOE_BLOCK * MOE_BLOCK
    pad_end = jnp.cumsum(padded)
    pad_start = pad_end - padded
    start = jnp.cumsum(counts) - counts
    dest_sorted = (pad_start[sorted_e] + jnp.arange(tk) - start[sorted_e]).astype(jnp.int32)
    row_tok = jnp.zeros((n_blocks * MOE_BLOCK,), jnp.int32).at[dest_sorted].set((order // TOP_K).astype(jnp.int32))
    dest = jnp.zeros((tk,), jnp.int32).at[order].set(dest_sorted).reshape(t, TOP_K)
    block_e = jnp.minimum(jnp.searchsorted(pad_end, jnp.arange(n_blocks) * MOE_BLOCK, side="right"),
                          n_experts - 1).astype(jnp.int32)
    n_used = (pad_end[-1:] // MOE_BLOCK).astype(jnp.int32)
    return gates, row_tok, dest, block_e, n_used


def kernel(x, w_in, w_in_vres, shift_mu, shift_mu_vres, rwkv_w0, rwkv_w2, rwkv_a0, rwkv_a2, rwkv_v0, rwkv_v2, rwkv_g2, rwkv_k_k, rwkv_k_a, rwkv_r_k, rwkv_gn_g, rwkv_gn_b, lru_conv_w, lru_conv_b, lru_wa, lru_ba, lru_wx, lru_bx, lru_lambda, conf_conv_w, conf_conv_b, conf_ln_g, conf_ln_b, w_branch, w_out, ln1_g, ln1_b, router_w, router_b, exp_w_gu, exp_b_gu, exp_w_down, exp_b_down, ln2_g, ln2_b):
    batch, seq, d = x.shape
    depth = w_in.shape[0]
    c = rwkv_w0.shape[-1]
    n_heads = c // HEAD_DIM
    n_rwkv = shift_mu.shape[-1]
    n_experts = router_w.shape[-1]
    t = batch * seq
    alpha = float((2 * depth) ** 0.25)
    vec = lambda a: a.reshape(1, -1)
    pad_rows = lambda w, lo, n: jnp.pad(w, ((lo, n - lo - w.shape[0]), (0, 0)))

    xf = x.reshape(t, d)
    v_first = None
    for l in range(depth):
        wl = w_in[l]
        d_lora = rwkv_w2.shape[1]
        d_vres = w_in_vres.shape[-1]
        w_rwkv = jnp.pad(wl[:, :n_rwkv], ((0, 0), (0, _C_VRES - n_rwkv)))
        mu = jnp.pad(shift_mu[l], (0, _C_VRES - n_rwkv))
        if l == 0:
            w_rwkv = jnp.pad(w_rwkv, ((0, 0), (0, RWKV_COLS - _C_VRES)))
            mu = jnp.pad(mu, (0, RWKV_COLS - _C_VRES))
        else:
            w_rwkv = jnp.concatenate(
                [w_rwkv, jnp.pad(w_in_vres[l - 1], ((0, 0), (0, LANE - d_vres)))], axis=1)
            mu = jnp.concatenate([mu, jnp.pad(shift_mu_vres[l - 1], (0, LANE - d_vres))])
        o = n_rwkv
        w_lru = wl[:, o:o + 2 * c]
        w_conf = wl[:, o + 2 * c:o + 4 * c]
        w_merge = wl[:, o + 4 * c:]
        xb = xf.astype(BF16)
        p_rwkv = _matmul(xb, w_rwkv.astype(BF16), F32, 1024, 512)
        p_lru = _matmul(xb, w_lru.astype(BF16), F32, 1024, 1024)
        p_conf = _matmul(xb, w_conf.astype(BF16), F32, 1024, 1024)
        gates = _matmul(xb, w_merge.astype(BF16), BF16, 1024, 1024, act="sigmoid")

        w2 = pad_rows(rwkv_w2[l], 0, LANE)
        a2 = pad_rows(rwkv_a2[l], d_lora, LANE)
        g2 = pad_rows(rwkv_g2[l], 0, 2 * LANE)
        vres = None
        if l > 0:
            vres = (vec(rwkv_v0[l - 1]), pad_rows(rwkv_v2[l - 1], 0, LANE), v_first)
        r_h, k_h, v_t, lw_h, alr_h, g_h = _rwkv_prep(
            p_rwkv, vec(mu), vec(rwkv_w0[l]), vec(rwkv_a0[l]), w2, a2, g2, vres, batch, seq)
        if l == 0:
            v_first = v_t
        v_h = v_t.reshape(batch, seq, n_heads, HEAD_DIM).transpose(0, 2, 1, 3)
        y_h = _rwkv_chunk(r_h, k_h, v_h, lw_h, alr_h, g_h, rwkv_k_k[l], rwkv_k_a[l],
                          rwkv_r_k[l].reshape(-1), rwkv_gn_g[l], rwkv_gn_b[l])
        y_a = y_h.transpose(0, 2, 1, 3).reshape(t, c)

        y_b = _lru(p_lru, lru_conv_w[l], vec(lru_conv_b[l]), lru_wa[l], vec(lru_ba[l]),
                   lru_wx[l], vec(lru_bx[l]), vec(lru_lambda[l]), batch, seq)
        y_c = _conformer(p_conf, conf_conv_w[l], vec(conf_conv_b[l]), vec(conf_ln_g[l]),
                         vec(conf_ln_b[l]), batch, seq)

        mixed = _merge(y_a, y_b, y_c, w_branch[l].astype(BF16), gates)
        x1, x1b, logits = _out_ln(mixed, xf, w_out[l].astype(BF16), vec(ln1_g[l]), vec(ln1_b[l]),
                                  router_w[l], vec(router_b[l]), alpha)

        gw, row_tok, dest, block_e, n_used = _route(logits[:, :n_experts], n_experts)
        xs = jnp.take(x1b, row_tok, axis=0)
        ys = _experts(xs, block_e, n_used, exp_w_gu[l].astype(BF16), exp_b_gu[l],
                      exp_w_down[l].astype(BF16), exp_b_down[l])
        yg = jnp.take(ys, dest.reshape(-1), axis=0).reshape(t, TOP_K * d)
        xf = _combine_ln(yg, gw, x1, vec(ln2_g[l]), vec(ln2_b[l]), alpha)
    return xf.reshape(batch, seq, d)
```

```python
import functools

import jax
import jax.numpy as jnp
from jax import lax
from jax.experimental import pallas as pl
from jax.experimental.pallas import tpu as pltpu

F32 = jnp.float32
BF16 = jnp.bfloat16

HEAD_DIM = 64
RWKV_CHUNK = 64
RWKV_GN_EPS = 64e-5
LRU_C = 8.0
LN_EPS = 1e-5
TOP_K = 4
MOE_BLOCK = 512
SWIGLU_ALPHA = 1.702
SWIGLU_LIMIT = 7.0
LANE = 128
VMEM_LIMIT = 56 * 1024 * 1024


def _cparams(*sem):
    return pltpu.CompilerParams(dimension_semantics=sem, vmem_limit_bytes=VMEM_LIMIT)


def _dot(a, b):
    return jnp.dot(a, b, preferred_element_type=F32)


def _dot_nt(a, b):
    return lax.dot_general(a, b, (((1,), (1,)), ((), ())), preferred_element_type=F32)


def _dot_tn(a, b):
    return lax.dot_general(a, b, (((0,), (0,)), ((), ())), preferred_element_type=F32)


def _split_hi_lo(w):
    hi = w.astype(BF16)
    lo = (w - hi.astype(F32)).astype(BF16)
    return hi, lo


def _dot3(a, b_hi, b_lo):
    a_hi, a_lo = _split_hi_lo(a)
    return _dot(a_hi, b_hi) + (_dot(a_lo, b_hi) + _dot(a_hi, b_lo))


def _softplus(z):
    return jnp.maximum(z, 0.0) + jnp.log(1.0 + jnp.exp(-jnp.abs(z)))


def _layer_norm(v, g, b):
    mu = jnp.mean(v, -1, keepdims=True)
    c = v - mu
    var = jnp.mean(c * c, -1, keepdims=True)
    return c * lax.rsqrt(var + LN_EPS) * g + b


def _cast_kernel(w_ref, o_ref):
    o_ref[...] = w_ref[...].astype(o_ref.dtype)


def _cast_layer(w, layer, tk=512):
    _, e, k, n = w.shape
    return pl.pallas_call(
        _cast_kernel,
        grid=(e, k // tk),
        in_specs=[pl.BlockSpec((None, None, tk, n), lambda i, j: (layer, i, j, 0))],
        out_specs=pl.BlockSpec((None, tk, n), lambda i, j: (i, j, 0)),
        out_shape=jax.ShapeDtypeStruct((e, k, n), BF16),
        compiler_params=_cparams("parallel", "parallel"),
        name="cast_bf16",
    )(w)


def _mm_kernel(x_ref, w_ref, o_ref, *, act):
    acc = _dot(x_ref[...], w_ref[...])
    if act == "sigmoid":
        acc = jax.nn.sigmoid(acc)
    o_ref[...] = acc.astype(o_ref.dtype)


def _matmul(x, w, out_dtype, tm, tn, act=None):
    m, k = x.shape
    n = w.shape[1]
    return pl.pallas_call(
        functools.partial(_mm_kernel, act=act),
        grid=(n // tn, m // tm),
        in_specs=[pl.BlockSpec((tm, k), lambda j, i: (i, 0)),
                  pl.BlockSpec((k, tn), lambda j, i: (0, j))],
        out_specs=pl.BlockSpec((tm, tn), lambda j, i: (i, j)),
        out_shape=jax.ShapeDtypeStruct((m, n), out_dtype),
        compiler_params=_cparams("parallel", "parallel"),
        name="in_proj",
    )(x, w)


RWKV_COLS = 3584
_C_LORA = 3072
_C_GATE = 3200
_C_VRES = 3456
_N_PREP_HM = 10


def _rwkv_prep_kernel(*refs, tiles_per_batch, has_vres, c_mix):
    n_in = 18 if has_vres else 14
    (p_ref, pprev_ref, mu_ref, w0_ref, a0_ref, w2h, w2l, a2h, a2l, g2_ref,
     kkw_ref, kaw_ref, rkw_ref, seg_ref) = refs[:14]
    if has_vres:
        v0_ref, v2h, v2l, vfirst_ref = refs[14:18]
    v_o = refs[n_in]
    at_o, bt_o, kt_o, rt_o, bp_o, kp_o, vh_o, ecl_o, bonus_o, g_o = refs[n_in + 1:]
    i = pl.program_id(0)
    tm = p_ref.shape[0]
    first = (i % tiles_per_batch) == 0

    def shifted(lo, hi):
        p = p_ref[:, lo:hi]
        prev_row = jnp.where(first, 0.0, pprev_ref[7:8, lo:hi])
        rows = lax.broadcasted_iota(jnp.int32, p.shape, 0)
        prev = jnp.where(rows == 0, prev_row, pltpu.roll(p, 1, axis=0))
        return p + (prev - p) * mu_ref[:, lo:hi]

    def put(o_ref, val):
        for h in range(c_mix // HEAD_DIM):
            o_ref[0, h] = val[:, h * HEAD_DIM:(h + 1) * HEAD_DIM].astype(o_ref.dtype)

    def head_sum(val):
        hi, lo = _split_hi_lo(val)
        seg = seg_ref[...]
        parts = []
        for j in range(c_mix // LANE):
            sl = slice(j * LANE, (j + 1) * LANE)
            parts.append(_dot(hi[:, sl], seg) + _dot(lo[:, sl], seg))
        return jnp.concatenate(parts, axis=1)

    r = shifted(0, c_mix)
    k_raw = shifted(c_mix, 2 * c_mix)
    v = shifted(2 * c_mix, 3 * c_mix)
    lora = shifted(_C_LORA, _C_LORA + LANE)
    w_log = -_softplus(-(w0_ref[...] + _dot3(jnp.tanh(lora), w2h[...], w2l[...]))) - 0.5
    lw = -jnp.exp(w_log)
    alr = jax.nn.sigmoid(a0_ref[...] + _dot3(lora, a2h[...], a2l[...]))
    gate_in = jax.nn.sigmoid(shifted(_C_GATE, _C_GATE + 2 * LANE))
    put(g_o, _dot(gate_in.astype(BF16), g2_ref[...]))
    if has_vres:
        pv = shifted(_C_VRES, _C_VRES + LANE)
        mix = jax.nn.sigmoid(v0_ref[...] + _dot3(pv, v2h[...], v2l[...]))
        v = v + (vfirst_ref[...] - v) * mix
    v_o[...] = v
    put(vh_o, v)

    kk = k_raw * kkw_ref[...]
    kk = kk / jnp.maximum(jnp.sqrt(jnp.maximum(head_sum(kk * kk), 0.0)), 1e-12)
    k = k_raw * (1.0 + (alr - 1.0) * kaw_ref[...])
    put(bonus_o, head_sum(r * k * rkw_ref[...]) * v)

    rows = lax.broadcasted_iota(jnp.int32, (tm, tm), 0)
    cols = lax.broadcasted_iota(jnp.int32, (tm, tm), 1)
    same = (rows // RWKV_CHUNK) == (cols // RWKV_CHUNK)
    tril = jnp.where(same & (rows >= cols), 1.0, 0.0).astype(BF16)
    ones = jnp.where(same, 1.0, 0.0).astype(BF16)
    l0 = lw.astype(BF16)
    rem = lw - l0.astype(F32)
    l1 = rem.astype(BF16)
    l2 = (rem - l1.astype(F32)).astype(BF16)
    cum = _dot(tril, l0) + (_dot(tril, l1) + _dot(tril, l2))
    tot = _dot(ones, l0) + (_dot(ones, l1) + _dot(ones, l2))

    e_neg = jnp.exp(-cum)
    e_tail = jnp.exp(tot - cum)
    b = kk * alr
    put(at_o, -kk * jnp.exp(cum - lw))
    put(bt_o, b * e_neg)
    put(kt_o, k * e_neg)
    put(rt_o, r * jnp.exp(cum))
    put(bp_o, b * e_tail)
    put(kp_o, k * e_tail)
    put(ecl_o, jnp.exp(tot))


def _rwkv_prep(p, mu, w0, a0, w2, a2, g2, k_k, k_a, r_k, vres, batch, seq, tm=256):
    t = p.shape[0]
    c_mix = w0.shape[-1]
    n_heads = c_mix // HEAD_DIM
    has_vres = vres is not None
    tiles_per_batch = seq // tm
    row = lambda i: (i, 0)
    const = lambda i: (0, 0)
    hm = lambda i: (i // tiles_per_batch, 0, i % tiles_per_batch, 0)
    vec = pl.BlockSpec((1, c_mix), const)
    lora_w = pl.BlockSpec((LANE, c_mix), const)
    lane_id = jnp.arange(LANE) // HEAD_DIM
    seg = (lane_id[:, None] == lane_id[None, :]).astype(BF16)
    in_specs = [
        pl.BlockSpec((tm, RWKV_COLS), row),
        pl.BlockSpec((8, RWKV_COLS), lambda i: (jnp.maximum(i * (tm // 8) - 1, 0), 0)),
        pl.BlockSpec((1, RWKV_COLS), const),
        vec, vec, lora_w, lora_w, lora_w, lora_w,
        pl.BlockSpec((2 * LANE, c_mix), const),
        vec, vec, vec, pl.BlockSpec((LANE, LANE), const),
    ]
    args = [p, p, mu, w0, a0, *_split_hi_lo(w2), *_split_hi_lo(a2), g2.astype(BF16), k_k, k_a, r_k, seg]
    if has_vres:
        v0, v2, v_first = vres
        in_specs += [vec, lora_w, lora_w, pl.BlockSpec((tm, c_mix), row)]
        args += [v0, *_split_hi_lo(v2), v_first]
    hm_spec = pl.BlockSpec((1, n_heads, tm, HEAD_DIM), hm)
    hm_shape = lambda dt: jax.ShapeDtypeStruct((batch, n_heads, seq, HEAD_DIM), dt)
    out_shape = [jax.ShapeDtypeStruct((t, c_mix), F32)] + [hm_shape(BF16)] * 7 + [hm_shape(F32)] * 3
    out_specs = [pl.BlockSpec((tm, c_mix), row)] + [hm_spec] * _N_PREP_HM
    return pl.pallas_call(
        functools.partial(_rwkv_prep_kernel, tiles_per_batch=tiles_per_batch,
                          has_vres=has_vres, c_mix=c_mix),
        grid=(t // tm,), in_specs=in_specs, out_specs=out_specs, out_shape=out_shape,
        compiler_params=_cparams("parallel"), name="rwkv_prep",
    )(*args)


def _rwkv_chunk_kernel(at_ref, bt_ref, kt_ref, rt_ref, bp_ref, kp_ref, v_ref, ecl_ref, bonus_ref,
                       g_ref, gng_ref, gnb_ref, y_ref, st_ref, *, heads):
    c = pl.program_id(2)
    L = RWKV_CHUNK
    n = HEAD_DIM

    @pl.when(c == 0)
    def _():
        st_ref[...] = jnp.zeros_like(st_ref)

    ti = lax.broadcasted_iota(jnp.int32, (L, L), 0)
    si = lax.broadcasted_iota(jnp.int32, (L, L), 1)
    strict = ti > si
    incl = ti >= si
    eye = ti == si
    eye_f = jnp.where(eye, 1.0, 0.0)
    bf = lambda t: t.astype(BF16)

    hs = range(heads)
    each = lambda f, *cols: [f(*xs) for xs in zip(*cols)]
    at = [at_ref[0, h] for h in hs]
    bt = [bt_ref[0, h] for h in hs]
    kt = [kt_ref[0, h] for h in hs]
    rt = [rt_ref[0, h] for h in hs]
    vb = [v_ref[0, h] for h in hs]
    bp = [bp_ref[0, h] for h in hs]
    kp = [kp_ref[0, h] for h in hs]
    ar = each(lambda a, r: jnp.concatenate([a, r], axis=0), at, rt)
    xb = each(_dot_nt, ar, bt)
    xk = each(_dot_nt, ar, kt)
    a_ab = [jnp.where(strict, x[:L], 0.0) for x in xb]
    a_ak = [bf(jnp.where(strict, x[:L], 0.0)) for x in xk]
    m_rb = [bf(jnp.where(incl, x[L:], 0.0)) for x in xb]
    m_rk = [bf(jnp.where(incl, x[L:], 0.0)) for x in xk]
    pw = a_ab
    tinv = [eye_f + a for a in a_ab]
    for _ in range(5):
        pwb = [bf(p) for p in pw]
        pw = each(_dot, pwb, pwb)
        tinv = each(lambda t, p: t + _dot(bf(t), bf(p)), tinv, pw)
    tb = [bf(t) for t in tinv]
    u = each(lambda a, v: bf(_dot(a, v)), a_ak, vb)
    e1 = each(lambda t, x: bf(_dot(t, x)), tb, u)
    ah = each(lambda t, a: bf(_dot(t, a)), tb, at)
    rh = each(lambda r, m, a: bf(r.astype(F32) + _dot(m, a)), rt, m_rb, ah)
    o1 = each(lambda mb, e, mk, v: _dot(mb, e) + _dot(mk, v), m_rb, e1, m_rk, vb)
    phi = [bf(jnp.where(eye, ecl_ref[0, h, 0:1, :], 0.0) + _dot_tn(bp[h], ah[h])) for h in hs]
    psi = each(lambda b, e, k, v: _dot_tn(b, e) + _dot_tn(k, v), bp, e1, kp, vb)
    stb = [bf(st_ref[h]) for h in hs]
    ys = each(lambda r, p, s: _dot(jnp.concatenate([r, p], axis=0), s), rh, phi, stb)
    for h in hs:
        st_ref[h] = ys[h][L:] + psi[h]
    outs = []
    for h in hs:
        y = ys[h][:L] + o1[h]
        mu = jnp.mean(y, -1, keepdims=True)
        yc = y - mu
        var = jnp.mean(yc * yc, -1, keepdims=True)
        yn = yc * lax.rsqrt(var + RWKV_GN_EPS) * gng_ref[h] + gnb_ref[h]
        outs.append((yn + bonus_ref[0, h]) * g_ref[0, h])
    y_ref[...] = jnp.concatenate(outs, axis=1).astype(y_ref.dtype)


def _rwkv_chunk(ops, gn_g, gn_b, heads_per_step=16):
    b, h, s, n = ops[0].shape
    hb = heads_per_step
    n_chunks = s // RWKV_CHUNK
    seq_spec = pl.BlockSpec((1, hb, RWKV_CHUNK, n), lambda bi, hi, ci: (bi, hi, ci, 0))
    par_spec = pl.BlockSpec((hb, 1, n), lambda bi, hi, ci: (hi, 0, 0))
    par = lambda w: w.reshape(h, 1, n)
    return pl.pallas_call(
        functools.partial(_rwkv_chunk_kernel, heads=hb),
        grid=(b, h // hb, n_chunks),
        in_specs=[seq_spec] * _N_PREP_HM + [par_spec] * 2,
        out_specs=pl.BlockSpec((RWKV_CHUNK, hb * n), lambda bi, hi, ci: (bi * n_chunks + ci, hi)),
        out_shape=jax.ShapeDtypeStruct((b * s, h * n), BF16),
        scratch_shapes=[pltpu.VMEM((hb, n, n), F32)],
        compiler_params=_cparams("parallel", "parallel", "arbitrary"),
        name="rwkv_chunk",
    )(*ops, par(gn_g), par(gn_b))


def _shift_rows(x, d, fill):
    rows = lax.broadcasted_iota(jnp.int32, x.shape, 0)
    return jnp.where(rows < d, fill, pltpu.roll(x, d, axis=0))


def _lru_kernel(gate_ref, x_ref, cw_ref, cb_ref, wa_ref, ba_ref, wx_ref, bx_ref, lam_ref,
                y_ref, ext_ref, h_ref, *, conv_w):
    ti = pl.program_id(1)
    tm = x_ref.shape[0]

    @pl.when(ti == 0)
    def _():
        ext_ref[0:8, :] = jnp.zeros((8, ext_ref.shape[1]), F32)
        h_ref[...] = jnp.zeros_like(h_ref)

    x = x_ref[...]
    ext_ref[8:8 + tm, :] = x
    xc = cb_ref[...] + cw_ref[conv_w - 1:conv_w, :] * x
    for d in range(1, conv_w):
        xc = xc + cw_ref[conv_w - 1 - d:conv_w - d, :] * ext_ref[8 - d:8 - d + tm, :]
    ext_ref[0:8, :] = x[tm - 8:tm, :]
    xb = xc.astype(BF16)

    def gate(w_ref, b_ref):
        parts = [_dot(xb[:, j * LANE:(j + 1) * LANE], w_ref[j]) for j in range(w_ref.shape[0])]
        return jax.nn.sigmoid(jnp.concatenate(parts, axis=1) + b_ref[...])

    rg = gate(wa_ref, ba_ref)
    ig = gate(wx_ref, bx_ref)
    log_a = -LRU_C * rg * _softplus(-lam_ref[...])
    a = jnp.exp(log_a)
    th = jnp.tanh(log_a)
    b = jnp.sqrt(-2.0 * th / (1.0 - th)) * (ig * xc)
    d = 1
    while d < tm:
        b = b + a * _shift_rows(b, d, 0.0)
        a = a * _shift_rows(a, d, 1.0)
        d *= 2
    h = a * h_ref[...] + b
    h_ref[...] = h[tm - 1:tm, :]
    y_ref[...] = (h * jax.nn.gelu(gate_ref[...], approximate=True)).astype(y_ref.dtype)


def _pair_blocks(w):
    h, n, _ = w.shape
    z = jnp.zeros((h // 2, n, n), w.dtype)
    top = jnp.concatenate([w[0::2], z], axis=2)
    bot = jnp.concatenate([z, w[1::2]], axis=2)
    return jnp.concatenate([top, bot], axis=1).astype(BF16)


def _lru(proj, conv_w, conv_b, wa, ba, wx, bx, lam, batch, seq, tm=256):
    c = conv_b.shape[-1]
    tiles = seq // tm
    row0 = lambda b, i: (b * tiles + i, 0)
    row1 = lambda b, i: (b * tiles + i, 1)
    const = lambda b, i: (0, 0)
    vec = pl.BlockSpec((1, c), const)
    width = conv_w.shape[0]
    blk = pl.BlockSpec((c // LANE, LANE, LANE), lambda b, i: (0, 0, 0))
    return pl.pallas_call(
        functools.partial(_lru_kernel, conv_w=width),
        grid=(batch, tiles),
        in_specs=[pl.BlockSpec((tm, c), row0), pl.BlockSpec((tm, c), row1),
                  pl.BlockSpec((width, c), const), vec, blk, vec, blk, vec, vec],
        out_specs=pl.BlockSpec((tm, c), row0),
        out_shape=jax.ShapeDtypeStruct((batch * seq, c), BF16),
        scratch_shapes=[pltpu.VMEM((tm + 8, c), F32), pltpu.VMEM((1, c), F32)],
        compiler_params=_cparams("parallel", "arbitrary"),
        name="rglru",
    )(proj, proj, conv_w, conv_b, _pair_blocks(wa), ba, _pair_blocks(wx), bx, lam)


_CONF_HALO = 32


def _conf_kernel(val_ref, gate_ref, cw_ref, cb_ref, g_ref, b_ref, y_ref, ext_ref, *, conv_w):
    ti = pl.program_id(1)
    tm = val_ref.shape[0]

    @pl.when(ti == 0)
    def _():
        ext_ref[0:_CONF_HALO, :] = jnp.zeros((_CONF_HALO, ext_ref.shape[1]), F32)

    u = val_ref[...] * jax.nn.sigmoid(gate_ref[...])
    ext_ref[_CONF_HALO:_CONF_HALO + tm, :] = u
    acc = cb_ref[...] + cw_ref[conv_w - 1:conv_w, :] * u
    for d in range(1, conv_w):
        acc = acc + cw_ref[conv_w - 1 - d:conv_w - d, :] * ext_ref[_CONF_HALO - d:_CONF_HALO - d + tm, :]
    ext_ref[0:_CONF_HALO, :] = u[tm - _CONF_HALO:tm, :]
    y = _layer_norm(acc, g_ref[...], b_ref[...])
    y_ref[...] = (y * jax.nn.sigmoid(y)).astype(y_ref.dtype)


def _conformer(proj, conv_w, conv_b, ln_g, ln_b, batch, seq, tm=256):
    c = conv_b.shape[-1]
    tiles = seq // tm
    row0 = lambda b, i: (b * tiles + i, 0)
    row1 = lambda b, i: (b * tiles + i, 1)
    const = lambda b, i: (0, 0)
    vec = pl.BlockSpec((1, c), const)
    width = conv_w.shape[0]
    return pl.pallas_call(
        functools.partial(_conf_kernel, conv_w=width),
        grid=(batch, tiles),
        in_specs=[pl.BlockSpec((tm, c), row0), pl.BlockSpec((tm, c), row1),
                  pl.BlockSpec((width, c), const), vec, vec, vec],
        out_specs=pl.BlockSpec((tm, c), row0),
        out_shape=jax.ShapeDtypeStruct((batch * seq, c), BF16),
        scratch_shapes=[pltpu.VMEM((tm + _CONF_HALO, c), F32)],
        compiler_params=_cparams("parallel", "arbitrary"),
        name="conformer",
    )(proj, proj, conv_w, conv_b, ln_g, ln_b)


def _merge_kernel(ya_ref, yb_ref, yc_ref, wb_ref, ga_ref, gb_ref, gc_ref, o_ref):
    acc = ga_ref[...].astype(F32) * _dot(ya_ref[...], wb_ref[0])
    acc = acc + gb_ref[...].astype(F32) * _dot(yb_ref[...], wb_ref[1])
    acc = acc + gc_ref[...].astype(F32) * _dot(yc_ref[...], wb_ref[2])
    o_ref[...] = acc.astype(o_ref.dtype)


def _merge(ya, yb, yc, w_branch, gates, tm=1024, tn=1024):
    t, c = ya.shape
    d = w_branch.shape[-1]
    nj = d // tn
    ysp = pl.BlockSpec((tm, c), lambda j, i: (i, 0))
    gsp = lambda k: pl.BlockSpec((tm, tn), lambda j, i: (i, k * nj + j))
    return pl.pallas_call(
        _merge_kernel,
        grid=(nj, t // tm),
        in_specs=[ysp, ysp, ysp, pl.BlockSpec((3, c, tn), lambda j, i: (0, 0, j)),
                  gsp(0), gsp(1), gsp(2)],
        out_specs=pl.BlockSpec((tm, tn), lambda j, i: (i, j)),
        out_shape=jax.ShapeDtypeStruct((t, d), BF16),
        compiler_params=_cparams("parallel", "parallel"),
        name="merge",
    )(ya, yb, yc, w_branch, gates, gates, gates)


def _out_ln_kernel(m_ref, x_ref, w_ref, g_ref, b_ref, rwh_ref, rwl_ref, rb_ref,
                   o_ref, ob_ref, lg_ref, *, alpha):
    y = alpha * x_ref[...] + _dot(m_ref[...], w_ref[...])
    y = _layer_norm(y, g_ref[...], b_ref[...])
    o_ref[...] = y
    ob_ref[...] = y.astype(BF16)
    lg_ref[...] = _dot3(y, rwh_ref[...], rwl_ref[...]) + rb_ref[...]


def _out_ln(mixed, x, w_out, ln_g, ln_b, router_w, router_b, alpha, tm=512):
    t, d = x.shape
    row = lambda i: (i, 0)
    const = lambda i: (0, 0)
    ne = router_w.shape[-1]
    rw = jnp.pad(router_w, ((0, 0), (0, LANE - ne)))
    rb = jnp.pad(router_b, ((0, 0), (0, LANE - ne)), constant_values=-1e30)
    return pl.pallas_call(
        functools.partial(_out_ln_kernel, alpha=alpha),
        grid=(t // tm,),
        in_specs=[pl.BlockSpec((tm, d), row), pl.BlockSpec((tm, d), row), pl.BlockSpec((d, d), const),
                  pl.BlockSpec((1, d), const), pl.BlockSpec((1, d), const),
                  pl.BlockSpec((d, LANE), const), pl.BlockSpec((d, LANE), const),
                  pl.BlockSpec((1, LANE), const)],
        out_specs=[pl.BlockSpec((tm, d), row), pl.BlockSpec((tm, d), row), pl.BlockSpec((tm, LANE), row)],
        out_shape=[jax.ShapeDtypeStruct((t, d), F32), jax.ShapeDtypeStruct((t, d), BF16),
                   jax.ShapeDtypeStruct((t, LANE), F32)],
        compiler_params=_cparams("parallel"),
        name="out_ln",
    )(mixed, x, w_out, ln_g, ln_b, *_split_hi_lo(rw), rb)


def _route_kernel(lg_ref, idx_o, rank_o, gate_o, cnt_o, base_ref):
    i = pl.program_id(0)
    tm = lg_ref.shape[0]

    @pl.when(i == 0)
    def _():
        base_ref[...] = jnp.zeros_like(base_ref)

    l = lg_ref[...]
    lane = lax.broadcasted_iota(jnp.int32, l.shape, 1)
    sels, vals, idxs = [], [], []
    for _ in range(TOP_K):
        m = jnp.max(l, -1, keepdims=True)
        idx = jnp.min(jnp.where(l == m, lane, LANE), -1, keepdims=True)
        sel = lane == idx
        l = jnp.where(sel, -jnp.inf, l)
        sels.append(sel)
        vals.append(m)
        idxs.append(idx)
    ex = [jnp.exp(v - vals[0]) for v in vals]
    den = ex[0] + ex[1] + ex[2] + ex[3]
    onehot = jnp.zeros(l.shape, F32)
    for sel in sels:
        onehot = onehot + jnp.where(sel, 1.0, 0.0)
    rows = lax.broadcasted_iota(jnp.int32, (tm, tm), 0)
    cols = lax.broadcasted_iota(jnp.int32, (tm, tm), 1)
    before = jnp.where(rows > cols, 1.0, 0.0).astype(BF16)
    seen = _dot(before, onehot.astype(BF16)) + base_ref[...]
    idx_out = jnp.zeros(l.shape, jnp.int32)
    rank_out = jnp.zeros(l.shape, jnp.int32)
    gate_out = jnp.zeros(l.shape, F32)
    for k in range(TOP_K):
        rank = jnp.sum(jnp.where(sels[k], seen, 0.0), -1, keepdims=True).astype(jnp.int32)
        idx_out = jnp.where(lane == k, idxs[k], idx_out)
        rank_out = jnp.where(lane == k, rank, rank_out)
        gate_out = jnp.where(lane == k, ex[k] / den, gate_out)
    idx_o[...] = idx_out
    rank_o[...] = rank_out
    gate_o[...] = gate_out
    base_ref[...] = base_ref[...] + jnp.sum(onehot, 0, keepdims=True)
    cnt_o[...] = base_ref[...]


def _route(logits, n_experts, tm=512):
    t = logits.shape[0]
    row = lambda i: (i, 0)
    tile = pl.BlockSpec((tm, LANE), row)
    idx, rank, gates, counts = pl.pallas_call(
        _route_kernel,
        grid=(t // tm,),
        in_specs=[tile],
        out_specs=[tile, tile, tile, pl.BlockSpec((1, LANE), lambda i: (0, 0))],
        out_shape=[jax.ShapeDtypeStruct((t, LANE), jnp.int32), jax.ShapeDtypeStruct((t, LANE), jnp.int32),
                   jax.ShapeDtypeStruct((t, LANE), F32), jax.ShapeDtypeStruct((1, LANE), F32)],
        scratch_shapes=[pltpu.VMEM((1, LANE), F32)],
        compiler_params=_cparams("arbitrary"),
        name="route",
    )(logits)
    idx, rank, gates = idx[:, :TOP_K], rank[:, :TOP_K], gates[:, :TOP_K]
    counts = counts[0, :n_experts].astype(jnp.int32)
    n_blocks = -(-(t * TOP_K) // MOE_BLOCK) + n_experts
    padded = (counts + MOE_BLOCK - 1) // MOE_BLOCK * MOE_BLOCK
    pad_end = jnp.cumsum(padded)
    pad_start = pad_end - padded
    start_of = jnp.sum(jnp.where(idx[..., None] == jnp.arange(n_experts), pad_start, 0), -1)
    dest = (start_of + rank).astype(jnp.int32)
    tok = jnp.broadcast_to(jnp.arange(t, dtype=jnp.int32)[:, None], dest.shape)
    row_tok = jnp.zeros((n_blocks * MOE_BLOCK,), jnp.int32).at[dest.reshape(-1)].set(tok.reshape(-1))
    block_start = jnp.arange(n_blocks, dtype=jnp.int32) * MOE_BLOCK
    block_e = jnp.minimum(jnp.sum(pad_end[None, :] <= block_start[:, None], -1), n_experts - 1)
    n_used = (pad_end[-1:] // MOE_BLOCK).astype(jnp.int32)
    return gates, row_tok, dest, block_e.astype(jnp.int32), n_used


def _expert_kernel(be_ref, nused_ref, x_ref, wgu_ref, bgu_ref, wd_ref, bd_ref, o_ref, *, d_ff):
    i = pl.program_id(0)

    @pl.when(i < nused_ref[0])
    def _():
        gu = _dot(x_ref[...], wgu_ref[0]) + bgu_ref[0]
        gate = jnp.minimum(gu[:, :d_ff], SWIGLU_LIMIT)
        up = jnp.clip(gu[:, d_ff:], -SWIGLU_LIMIT, SWIGLU_LIMIT)
        h = (up + 1.0) * gate * jax.nn.sigmoid(SWIGLU_ALPHA * gate)
        o_ref[...] = (_dot(h.astype(BF16), wd_ref[0]) + bd_ref[0]).astype(o_ref.dtype)

    @pl.when(i >= nused_ref[0])
    def _():
        o_ref[...] = jnp.zeros_like(o_ref)


def _experts(xs, block_e, n_used, w_gu, b_gu, w_down, b_down):
    rows, d = xs.shape
    n_blocks = rows // MOE_BLOCK
    ne, d_ff, _ = w_down.shape
    xrow = lambda i, be, nu: (jnp.minimum(i, nu[0] - 1), 0)
    grid_spec = pltpu.PrefetchScalarGridSpec(
        num_scalar_prefetch=2,
        grid=(n_blocks,),
        in_specs=[pl.BlockSpec((MOE_BLOCK, d), xrow),
                  pl.BlockSpec((1, d, 2 * d_ff), lambda i, be, nu: (be[i], 0, 0)),
                  pl.BlockSpec((1, 1, 2 * d_ff), lambda i, be, nu: (be[i], 0, 0)),
                  pl.BlockSpec((1, d_ff, d), lambda i, be, nu: (be[i], 0, 0)),
                  pl.BlockSpec((1, 1, d), lambda i, be, nu: (be[i], 0, 0))],
        out_specs=pl.BlockSpec((MOE_BLOCK, d), lambda i, be, nu: (i, 0)),
    )
    return pl.pallas_call(
        functools.partial(_expert_kernel, d_ff=d_ff),
        grid_spec=grid_spec,
        out_shape=jax.ShapeDtypeStruct((rows, d), BF16),
        compiler_params=_cparams("arbitrary"),
        name="experts",
    )(block_e, n_used, xs, w_gu, b_gu.reshape(ne, 1, 2 * d_ff), w_down, b_down.reshape(ne, 1, d))


def _combine_ln_kernel(yg_ref, gw_ref, x_ref, g_ref, b_ref, o_ref, ob_ref, *, alpha, d):
    acc = alpha * x_ref[...]
    for k in range(TOP_K):
        acc = acc + gw_ref[:, k:k + 1] * yg_ref[:, k * d:(k + 1) * d].astype(F32)
    y = _layer_norm(acc, g_ref[...], b_ref[...])
    o_ref[...] = y
    ob_ref[...] = y.astype(BF16)


def _combine_ln(yg, gw, x, ln_g, ln_b, alpha, tm=512):
    t, d = x.shape
    row = lambda i: (i, 0)
    const = lambda i: (0, 0)
    return pl.pallas_call(
        functools.partial(_combine_ln_kernel, alpha=alpha, d=d),
        grid=(t // tm,),
        in_specs=[pl.BlockSpec((tm, TOP_K * d), row), pl.BlockSpec((tm, TOP_K), row),
                  pl.BlockSpec((tm, d), row), pl.BlockSpec((1, d), const), pl.BlockSpec((1, d), const)],
        out_specs=[pl.BlockSpec((tm, d), row), pl.BlockSpec((tm, d), row)],
        out_shape=[jax.ShapeDtypeStruct((t, d), F32), jax.ShapeDtypeStruct((t, d), BF16)],
        compiler_params=_cparams("parallel"),
        name="combine_ln",
    )(yg, gw, x, ln_g, ln_b)


def kernel(x, w_in, w_in_vres, shift_mu, shift_mu_vres, rwkv_w0, rwkv_w2, rwkv_a0, rwkv_a2, rwkv_v0, rwkv_v2, rwkv_g2, rwkv_k_k, rwkv_k_a, rwkv_r_k, rwkv_gn_g, rwkv_gn_b, lru_conv_w, lru_conv_b, lru_wa, lru_ba, lru_wx, lru_bx, lru_lambda, conf_conv_w, conf_conv_b, conf_ln_g, conf_ln_b, w_branch, w_out, ln1_g, ln1_b, router_w, router_b, exp_w_gu, exp_b_gu, exp_w_down, exp_b_down, ln2_g, ln2_b):
    batch, seq, d = x.shape
    depth = w_in.shape[0]
    c = rwkv_w0.shape[-1]
    n_rwkv = shift_mu.shape[-1]
    n_experts = router_w.shape[-1]
    t = batch * seq
    alpha = float((2 * depth) ** 0.25)
    vec = lambda a: a.reshape(1, -1)
    pad_rows = lambda w, lo, n: jnp.pad(w, ((lo, n - lo - w.shape[0]), (0, 0)))

    xf = x.reshape(t, d)
    xb = xf.astype(BF16)
    v_first = None
    for l in range(depth):
        wl = w_in[l].astype(BF16)
        d_lora = rwkv_w2.shape[1]
        d_vres = w_in_vres.shape[-1]
        w_rwkv = jnp.pad(wl[:, :n_rwkv], ((0, 0), (0, _C_VRES - n_rwkv)))
        mu = jnp.pad(shift_mu[l], (0, _C_VRES - n_rwkv))
        if l == 0:
            w_rwkv = jnp.pad(w_rwkv, ((0, 0), (0, RWKV_COLS - _C_VRES)))
            mu = jnp.pad(mu, (0, RWKV_COLS - _C_VRES))
        else:
            w_rwkv = jnp.concatenate(
                [w_rwkv, jnp.pad(w_in_vres[l - 1].astype(BF16), ((0, 0), (0, LANE - d_vres)))], axis=1)
            mu = jnp.concatenate([mu, jnp.pad(shift_mu_vres[l - 1], (0, LANE - d_vres))])
        o = n_rwkv
        p_rwkv = _matmul(xb, w_rwkv, F32, 1024, 512)
        p_lru = _matmul(xb, wl[:, o:o + 2 * c], F32, 1024, 1024)
        p_conf = _matmul(xb, wl[:, o + 2 * c:o + 4 * c], F32, 1024, 1024)
        gates = _matmul(xb, wl[:, o + 4 * c:], BF16, 1024, 1024, act="sigmoid")

        w2 = pad_rows(rwkv_w2[l], 0, LANE)
        a2 = pad_rows(rwkv_a2[l], d_lora, LANE)
        g2 = pad_rows(rwkv_g2[l], 0, 2 * LANE)
        vres = None
        if l > 0:
            vres = (vec(rwkv_v0[l - 1]), pad_rows(rwkv_v2[l - 1], 0, LANE), v_first)
        v_t, *chunk_ops = _rwkv_prep(
            p_rwkv, vec(mu), vec(rwkv_w0[l]), vec(rwkv_a0[l]), w2, a2, g2,
            vec(rwkv_k_k[l]), vec(rwkv_k_a[l]), vec(rwkv_r_k[l]), vres, batch, seq)
        if l == 0:
            v_first = v_t
        y_a = _rwkv_chunk(chunk_ops, rwkv_gn_g[l], rwkv_gn_b[l])

        y_b = _lru(p_lru, lru_conv_w[l], vec(lru_conv_b[l]), lru_wa[l], vec(lru_ba[l]),
                   lru_wx[l], vec(lru_bx[l]), vec(lru_lambda[l]), batch, seq)
        y_c = _conformer(p_conf, conf_conv_w[l], vec(conf_conv_b[l]), vec(conf_ln_g[l]),
                         vec(conf_ln_b[l]), batch, seq)

        mixed = _merge(y_a, y_b, y_c, w_branch[l].astype(BF16), gates)
        x1, x1b, logits = _out_ln(mixed, xf, w_out[l].astype(BF16), vec(ln1_g[l]), vec(ln1_b[l]),
                                  router_w[l], vec(router_b[l]), alpha)

        gw, row_tok, dest, block_e, n_used = _route(logits, n_experts)
        xs = jnp.take(x1b, row_tok, axis=0)
        ys = _experts(xs, block_e, n_used, _cast_layer(exp_w_gu, l), exp_b_gu[l],
                      _cast_layer(exp_w_down, l), exp_b_down[l])
        yg = jnp.take(ys, dest.reshape(-1), axis=0).reshape(t, TOP_K * d)
        xf, xb = _combine_ln(yg, gw, x1, vec(ln2_g[l]), vec(ln2_b[l]), alpha)
    return xf.reshape(batch, seq, d)
```

```python
import functools

import jax
import jax.numpy as jnp
from jax import lax
from jax.experimental import pallas as pl
from jax.experimental.pallas import tpu as pltpu

F32 = jnp.float32
BF16 = jnp.bfloat16

HEAD_DIM = 64
RWKV_CHUNK = 64
RWKV_GN_EPS = 64e-5
LRU_C = 8.0
LN_EPS = 1e-5
TOP_K = 4
MOE_BLOCK = 512
SWIGLU_ALPHA = 1.702
SWIGLU_LIMIT = 7.0
LANE = 128
VMEM_LIMIT = 56 * 1024 * 1024


def _cparams(*sem):
    return pltpu.CompilerParams(dimension_semantics=sem, vmem_limit_bytes=VMEM_LIMIT)


def _dot(a, b):
    return jnp.dot(a, b, preferred_element_type=F32)


def _dot_nt(a, b):
    return lax.dot_general(a, b, (((1,), (1,)), ((), ())), preferred_element_type=F32)


def _dot_tn(a, b):
    return lax.dot_general(a, b, (((0,), (0,)), ((), ())), preferred_element_type=F32)


def _split_hi_lo(w):
    hi = w.astype(BF16)
    lo = (w - hi.astype(F32)).astype(BF16)
    return hi, lo


def _dot3(a, b_hi, b_lo):
    a_hi, a_lo = _split_hi_lo(a)
    return _dot(a_hi, b_hi) + (_dot(a_lo, b_hi) + _dot(a_hi, b_lo))


def _softplus(z):
    return jnp.maximum(z, 0.0) + jnp.log(1.0 + jnp.exp(-jnp.abs(z)))


def _layer_norm(v, g, b):
    mu = jnp.mean(v, -1, keepdims=True)
    c = v - mu
    var = jnp.mean(c * c, -1, keepdims=True)
    return c * lax.rsqrt(var + LN_EPS) * g + b


def _mm_kernel(x_ref, w_ref, o_ref, *, act):
    acc = _dot(x_ref[...], w_ref[...])
    if act == "sigmoid":
        acc = jax.nn.sigmoid(acc)
    o_ref[...] = acc.astype(o_ref.dtype)


def _matmul(x, w, out_dtype, tm, tn, act=None):
    m, k = x.shape
    n = w.shape[1]
    return pl.pallas_call(
        functools.partial(_mm_kernel, act=act),
        grid=(n // tn, m // tm),
        in_specs=[pl.BlockSpec((tm, k), lambda j, i: (i, 0)),
                  pl.BlockSpec((k, tn), lambda j, i: (0, j))],
        out_specs=pl.BlockSpec((tm, tn), lambda j, i: (i, j)),
        out_shape=jax.ShapeDtypeStruct((m, n), out_dtype),
        compiler_params=_cparams("parallel", "parallel"),
        name="in_proj",
    )(x, w)


RWKV_COLS = 3584
_C_LORA = 3072
_C_GATE = 3200
_C_VRES = 3456
_N_PREP_HM = 10


def _rwkv_prep_kernel(*refs, tiles_per_batch, has_vres, c_mix):
    n_in = 18 if has_vres else 14
    (p_ref, pprev_ref, mu_ref, w0_ref, a0_ref, w2h, w2l, a2h, a2l, g2_ref,
     kkw_ref, kaw_ref, rkw_ref, seg_ref) = refs[:14]
    if has_vres:
        v0_ref, v2h, v2l, vfirst_ref = refs[14:18]
    v_o = refs[n_in]
    at_o, bt_o, kt_o, rt_o, bp_o, kp_o, vh_o, ecl_o, bonus_o, g_o = refs[n_in + 1:]
    i = pl.program_id(0)
    tm = p_ref.shape[0]
    first = (i % tiles_per_batch) == 0

    def shifted(lo, hi):
        p = p_ref[:, lo:hi]
        prev_row = jnp.where(first, 0.0, pprev_ref[7:8, lo:hi])
        rows = lax.broadcasted_iota(jnp.int32, p.shape, 0)
        prev = jnp.where(rows == 0, prev_row, pltpu.roll(p, 1, axis=0))
        return p + (prev - p) * mu_ref[:, lo:hi]

    def put(o_ref, val):
        for h in range(c_mix // HEAD_DIM):
            o_ref[0, h] = val[:, h * HEAD_DIM:(h + 1) * HEAD_DIM].astype(o_ref.dtype)

    def head_sum(val):
        hi, lo = _split_hi_lo(val)
        seg = seg_ref[...]
        parts = []
        for j in range(c_mix // LANE):
            sl = slice(j * LANE, (j + 1) * LANE)
            parts.append(_dot(hi[:, sl], seg) + _dot(lo[:, sl], seg))
        return jnp.concatenate(parts, axis=1)

    r = shifted(0, c_mix)
    k_raw = shifted(c_mix, 2 * c_mix)
    v = shifted(2 * c_mix, 3 * c_mix)
    lora = shifted(_C_LORA, _C_LORA + LANE)
    w_log = -_softplus(-(w0_ref[...] + _dot3(jnp.tanh(lora), w2h[...], w2l[...]))) - 0.5
    lw = -jnp.exp(w_log)
    alr = jax.nn.sigmoid(a0_ref[...] + _dot3(lora, a2h[...], a2l[...]))
    gate_in = jax.nn.sigmoid(shifted(_C_GATE, _C_GATE + 2 * LANE))
    put(g_o, _dot(gate_in.astype(BF16), g2_ref[...]))
    if has_vres:
        pv = shifted(_C_VRES, _C_VRES + LANE)
        mix = jax.nn.sigmoid(v0_ref[...] + _dot3(pv, v2h[...], v2l[...]))
        v = v + (vfirst_ref[...] - v) * mix
    v_o[...] = v
    put(vh_o, v)

    kk = k_raw * kkw_ref[...]
    kk = kk / jnp.maximum(jnp.sqrt(jnp.maximum(head_sum(kk * kk), 0.0)), 1e-12)
    k = k_raw * (1.0 + (alr - 1.0) * kaw_ref[...])
    put(bonus_o, head_sum(r * k * rkw_ref[...]) * v)

    rows = lax.broadcasted_iota(jnp.int32, (tm, tm), 0)
    cols = lax.broadcasted_iota(jnp.int32, (tm, tm), 1)
    same = (rows // RWKV_CHUNK) == (cols // RWKV_CHUNK)
    tril = jnp.where(same & (rows >= cols), 1.0, 0.0).astype(BF16)
    ones = jnp.where(same, 1.0, 0.0).astype(BF16)
    l0 = lw.astype(BF16)
    rem = lw - l0.astype(F32)
    l1 = rem.astype(BF16)
    l2 = (rem - l1.astype(F32)).astype(BF16)
    cum = _dot(tril, l0) + (_dot(tril, l1) + _dot(tril, l2))
    tot = _dot(ones, l0) + (_dot(ones, l1) + _dot(ones, l2))

    e_neg = jnp.exp(-cum)
    e_tail = jnp.exp(tot - cum)
    b = kk * alr
    put(at_o, -kk * jnp.exp(cum - lw))
    put(bt_o, b * e_neg)
    put(kt_o, k * e_neg)
    put(rt_o, r * jnp.exp(cum))
    put(bp_o, b * e_tail)
    put(kp_o, k * e_tail)
    put(ecl_o, jnp.exp(tot))


def _rwkv_prep(p, mu, w0, a0, w2, a2, g2, k_k, k_a, r_k, vres, batch, seq, tm=256):
    t = p.shape[0]
    c_mix = w0.shape[-1]
    n_heads = c_mix // HEAD_DIM
    has_vres = vres is not None
    tiles_per_batch = seq // tm
    row = lambda i: (i, 0)
    const = lambda i: (0, 0)
    hm = lambda i: (i // tiles_per_batch, 0, i % tiles_per_batch, 0)
    vec = pl.BlockSpec((1, c_mix), const)
    lora_w = pl.BlockSpec((LANE, c_mix), const)
    lane_id = jnp.arange(LANE) // HEAD_DIM
    seg = (lane_id[:, None] == lane_id[None, :]).astype(BF16)
    in_specs = [
        pl.BlockSpec((tm, RWKV_COLS), row),
        pl.BlockSpec((8, RWKV_COLS), lambda i: (jnp.maximum(i * (tm // 8) - 1, 0), 0)),
        pl.BlockSpec((1, RWKV_COLS), const),
        vec, vec, lora_w, lora_w, lora_w, lora_w,
        pl.BlockSpec((2 * LANE, c_mix), const),
        vec, vec, vec, pl.BlockSpec((LANE, LANE), const),
    ]
    args = [p, p, mu, w0, a0, *_split_hi_lo(w2), *_split_hi_lo(a2), g2.astype(BF16), k_k, k_a, r_k, seg]
    if has_vres:
        v0, v2, v_first = vres
        in_specs += [vec, lora_w, lora_w, pl.BlockSpec((tm, c_mix), row)]
        args += [v0, *_split_hi_lo(v2), v_first]
    hm_spec = pl.BlockSpec((1, n_heads, tm, HEAD_DIM), hm)
    hm_shape = lambda dt: jax.ShapeDtypeStruct((batch, n_heads, seq, HEAD_DIM), dt)
    out_shape = [jax.ShapeDtypeStruct((t, c_mix), F32)] + [hm_shape(BF16)] * 7 + [hm_shape(F32)] * 3
    out_specs = [pl.BlockSpec((tm, c_mix), row)] + [hm_spec] * _N_PREP_HM
    return pl.pallas_call(
        functools.partial(_rwkv_prep_kernel, tiles_per_batch=tiles_per_batch,
                          has_vres=has_vres, c_mix=c_mix),
        grid=(t // tm,), in_specs=in_specs, out_specs=out_specs, out_shape=out_shape,
        compiler_params=_cparams("parallel"), name="rwkv_prep",
    )(*args)


def _rwkv_chunk_kernel(at_ref, bt_ref, kt_ref, rt_ref, bp_ref, kp_ref, v_ref, ecl_ref, bonus_ref,
                       g_ref, gng_ref, gnb_ref, y_ref, st_ref, *, heads):
    c = pl.program_id(2)
    L = RWKV_CHUNK
    n = HEAD_DIM

    @pl.when(c == 0)
    def _():
        st_ref[...] = jnp.zeros_like(st_ref)

    ti = lax.broadcasted_iota(jnp.int32, (L, L), 0)
    si = lax.broadcasted_iota(jnp.int32, (L, L), 1)
    strict = ti > si
    incl = ti >= si
    eye = ti == si
    eye_f = jnp.where(eye, 1.0, 0.0)
    bf = lambda t: t.astype(BF16)

    hs = range(heads)
    each = lambda f, *cols: [f(*xs) for xs in zip(*cols)]
    at = [at_ref[0, h] for h in hs]
    bt = [bt_ref[0, h] for h in hs]
    kt = [kt_ref[0, h] for h in hs]
    rt = [rt_ref[0, h] for h in hs]
    vb = [v_ref[0, h] for h in hs]
    bp = [bp_ref[0, h] for h in hs]
    kp = [kp_ref[0, h] for h in hs]
    ar = each(lambda a, r: jnp.concatenate([a, r], axis=0), at, rt)
    xb = each(_dot_nt, ar, bt)
    xk = each(_dot_nt, ar, kt)
    a_ab = [jnp.where(strict, x[:L], 0.0) for x in xb]
    a_ak = [bf(jnp.where(strict, x[:L], 0.0)) for x in xk]
    m_rb = [bf(jnp.where(incl, x[L:], 0.0)) for x in xb]
    m_rk = [bf(jnp.where(incl, x[L:], 0.0)) for x in xk]
    pw = a_ab
    tinv = [eye_f + a for a in a_ab]
    for _ in range(5):
        pwb = [bf(p) for p in pw]
        pw = each(_dot, pwb, pwb)
        tinv = each(lambda t, p: t + _dot(bf(t), bf(p)), tinv, pw)
    tb = [bf(t) for t in tinv]
    u = each(lambda a, v: bf(_dot(a, v)), a_ak, vb)
    e1 = each(lambda t, x: bf(_dot(t, x)), tb, u)
    ah = each(lambda t, a: bf(_dot(t, a)), tb, at)
    rh = each(lambda r, m, a: bf(r.astype(F32) + _dot(m, a)), rt, m_rb, ah)
    o1 = each(lambda mb, e, mk, v: _dot(mb, e) + _dot(mk, v), m_rb, e1, m_rk, vb)
    phi = [bf(jnp.where(eye, ecl_ref[0, h, 0:1, :], 0.0) + _dot_tn(bp[h], ah[h])) for h in hs]
    psi = each(lambda b, e, k, v: _dot_tn(b, e) + _dot_tn(k, v), bp, e1, kp, vb)
    stb = [bf(st_ref[h]) for h in hs]
    ys = each(lambda r, p, s: _dot(jnp.concatenate([r, p], axis=0), s), rh, phi, stb)
    for h in hs:
        st_ref[h] = ys[h][L:] + psi[h]
    outs = []
    for h in hs:
        y = ys[h][:L] + o1[h]
        mu = jnp.mean(y, -1, keepdims=True)
        yc = y - mu
        var = jnp.mean(yc * yc, -1, keepdims=True)
        yn = yc * lax.rsqrt(var + RWKV_GN_EPS) * gng_ref[h] + gnb_ref[h]
        outs.append((yn + bonus_ref[0, h]) * g_ref[0, h])
    y_ref[...] = jnp.concatenate(outs, axis=1).astype(y_ref.dtype)


def _rwkv_chunk(ops, gn_g, gn_b, heads_per_step=16):
    b, h, s, n = ops[0].shape
    hb = heads_per_step
    n_chunks = s // RWKV_CHUNK
    seq_spec = pl.BlockSpec((1, hb, RWKV_CHUNK, n), lambda bi, hi, ci: (bi, hi, ci, 0))
    par_spec = pl.BlockSpec((hb, 1, n), lambda bi, hi, ci: (hi, 0, 0))
    par = lambda w: w.reshape(h, 1, n)
    return pl.pallas_call(
        functools.partial(_rwkv_chunk_kernel, heads=hb),
        grid=(b, h // hb, n_chunks),
        in_specs=[seq_spec] * _N_PREP_HM + [par_spec] * 2,
        out_specs=pl.BlockSpec((RWKV_CHUNK, hb * n), lambda bi, hi, ci: (bi * n_chunks + ci, hi)),
        out_shape=jax.ShapeDtypeStruct((b * s, h * n), BF16),
        scratch_shapes=[pltpu.VMEM((hb, n, n), F32)],
        compiler_params=_cparams("parallel", "parallel", "arbitrary"),
        name="rwkv_chunk",
    )(*ops, par(gn_g), par(gn_b))


def _shift_rows(x, d, fill):
    rows = lax.broadcasted_iota(jnp.int32, x.shape, 0)
    return jnp.where(rows < d, fill, pltpu.roll(x, d, axis=0))


def _lru_kernel(gate_ref, x_ref, cw_ref, cb_ref, wa_ref, ba_ref, wx_ref, bx_ref, lam_ref,
                y_ref, ext_ref, h_ref, *, conv_w):
    ti = pl.program_id(1)
    tm = x_ref.shape[0]

    @pl.when(ti == 0)
    def _():
        ext_ref[0:8, :] = jnp.zeros((8, ext_ref.shape[1]), F32)
        h_ref[...] = jnp.zeros_like(h_ref)

    x = x_ref[...]
    ext_ref[8:8 + tm, :] = x
    xc = cb_ref[...] + cw_ref[conv_w - 1:conv_w, :] * x
    for d in range(1, conv_w):
        xc = xc + cw_ref[conv_w - 1 - d:conv_w - d, :] * ext_ref[8 - d:8 - d + tm, :]
    ext_ref[0:8, :] = x[tm - 8:tm, :]
    xb = xc.astype(BF16)

    def gate(w_ref, b_ref):
        parts = [_dot(xb[:, j * LANE:(j + 1) * LANE], w_ref[j]) for j in range(w_ref.shape[0])]
        return jax.nn.sigmoid(jnp.concatenate(parts, axis=1) + b_ref[...])

    rg = gate(wa_ref, ba_ref)
    ig = gate(wx_ref, bx_ref)
    log_a = -LRU_C * rg * _softplus(-lam_ref[...])
    a = jnp.exp(log_a)
    th = jnp.tanh(log_a)
    b = jnp.sqrt(-2.0 * th / (1.0 - th)) * (ig * xc)
    d = 1
    while d < tm:
        b = b + a * _shift_rows(b, d, 0.0)
        a = a * _shift_rows(a, d, 1.0)
        d *= 2
    h = a * h_ref[...] + b
    h_ref[...] = h[tm - 1:tm, :]
    y_ref[...] = (h * jax.nn.gelu(gate_ref[...], approximate=True)).astype(y_ref.dtype)


def _pair_blocks(w):
    h, n, _ = w.shape
    z = jnp.zeros((h // 2, n, n), w.dtype)
    top = jnp.concatenate([w[0::2], z], axis=2)
    bot = jnp.concatenate([z, w[1::2]], axis=2)
    return jnp.concatenate([top, bot], axis=1).astype(BF16)


def _lru(proj, conv_w, conv_b, wa, ba, wx, bx, lam, batch, seq, tm=256):
    c = conv_b.shape[-1]
    tiles = seq // tm
    row0 = lambda b, i: (b * tiles + i, 0)
    row1 = lambda b, i: (b * tiles + i, 1)
    const = lambda b, i: (0, 0)
    vec = pl.BlockSpec((1, c), const)
    width = conv_w.shape[0]
    blk = pl.BlockSpec((c // LANE, LANE, LANE), lambda b, i: (0, 0, 0))
    return pl.pallas_call(
        functools.partial(_lru_kernel, conv_w=width),
        grid=(batch, tiles),
        in_specs=[pl.BlockSpec((tm, c), row0), pl.BlockSpec((tm, c), row1),
                  pl.BlockSpec((width, c), const), vec, blk, vec, blk, vec, vec],
        out_specs=pl.BlockSpec((tm, c), row0),
        out_shape=jax.ShapeDtypeStruct((batch * seq, c), BF16),
        scratch_shapes=[pltpu.VMEM((tm + 8, c), F32), pltpu.VMEM((1, c), F32)],
        compiler_params=_cparams("parallel", "arbitrary"),
        name="rglru",
    )(proj, proj, conv_w, conv_b, _pair_blocks(wa), ba, _pair_blocks(wx), bx, lam)


_CONF_HALO = 32


def _conf_kernel(val_ref, gate_ref, cw_ref, cb_ref, g_ref, b_ref, y_ref, ext_ref, *, conv_w):
    ti = pl.program_id(1)
    tm = val_ref.shape[0]

    @pl.when(ti == 0)
    def _():
        ext_ref[0:_CONF_HALO, :] = jnp.zeros((_CONF_HALO, ext_ref.shape[1]), F32)

    u = val_ref[...] * jax.nn.sigmoid(gate_ref[...])
    ext_ref[_CONF_HALO:_CONF_HALO + tm, :] = u
    acc = cb_ref[...] + cw_ref[conv_w - 1:conv_w, :] * u
    for d in range(1, conv_w):
        acc = acc + cw_ref[conv_w - 1 - d:conv_w - d, :] * ext_ref[_CONF_HALO - d:_CONF_HALO - d + tm, :]
    ext_ref[0:_CONF_HALO, :] = u[tm - _CONF_HALO:tm, :]
    y = _layer_norm(acc, g_ref[...], b_ref[...])
    y_ref[...] = (y * jax.nn.sigmoid(y)).astype(y_ref.dtype)


def _conformer(proj, conv_w, conv_b, ln_g, ln_b, batch, seq, tm=256):
    c = conv_b.shape[-1]
    tiles = seq // tm
    row0 = lambda b, i: (b * tiles + i, 0)
    row1 = lambda b, i: (b * tiles + i, 1)
    const = lambda b, i: (0, 0)
    vec = pl.BlockSpec((1, c), const)
    width = conv_w.shape[0]
    return pl.pallas_call(
        functools.partial(_conf_kernel, conv_w=width),
        grid=(batch, tiles),
        in_specs=[pl.BlockSpec((tm, c), row0), pl.BlockSpec((tm, c), row1),
                  pl.BlockSpec((width, c), const), vec, vec, vec],
        out_specs=pl.BlockSpec((tm, c), row0),
        out_shape=jax.ShapeDtypeStruct((batch * seq, c), BF16),
        scratch_shapes=[pltpu.VMEM((tm + _CONF_HALO, c), F32)],
        compiler_params=_cparams("parallel", "arbitrary"),
        name="conformer",
    )(proj, proj, conv_w, conv_b, ln_g, ln_b)


def _merge_kernel(ya_ref, yb_ref, yc_ref, wb_ref, ga_ref, gb_ref, gc_ref, o_ref):
    acc = ga_ref[...].astype(F32) * _dot(ya_ref[...], wb_ref[0])
    acc = acc + gb_ref[...].astype(F32) * _dot(yb_ref[...], wb_ref[1])
    acc = acc + gc_ref[...].astype(F32) * _dot(yc_ref[...], wb_ref[2])
    o_ref[...] = acc.astype(o_ref.dtype)


def _merge(ya, yb, yc, w_branch, gates, tm=1024, tn=1024):
    t, c = ya.shape
    d = w_branch.shape[-1]
    nj = d // tn
    ysp = pl.BlockSpec((tm, c), lambda j, i: (i, 0))
    gsp = lambda k: pl.BlockSpec((tm, tn), lambda j, i: (i, k * nj + j))
    return pl.pallas_call(
        _merge_kernel,
        grid=(nj, t // tm),
        in_specs=[ysp, ysp, ysp, pl.BlockSpec((3, c, tn), lambda j, i: (0, 0, j)),
                  gsp(0), gsp(1), gsp(2)],
        out_specs=pl.BlockSpec((tm, tn), lambda j, i: (i, j)),
        out_shape=jax.ShapeDtypeStruct((t, d), BF16),
        compiler_params=_cparams("parallel", "parallel"),
        name="merge",
    )(ya, yb, yc, w_branch, gates, gates, gates)


def _out_ln_kernel(m_ref, x_ref, w_ref, g_ref, b_ref, rwh_ref, rwl_ref, rb_ref,
                   o_ref, ob_ref, lg_ref, *, alpha):
    y = alpha * x_ref[...] + _dot(m_ref[...], w_ref[...])
    y = _layer_norm(y, g_ref[...], b_ref[...])
    o_ref[...] = y
    ob_ref[...] = y.astype(BF16)
    lg_ref[...] = _dot3(y, rwh_ref[...], rwl_ref[...]) + rb_ref[...]


def _out_ln(mixed, x, w_out, ln_g, ln_b, router_w, router_b, alpha, tm=512):
    t, d = x.shape
    row = lambda i: (i, 0)
    const = lambda i: (0, 0)
    ne = router_w.shape[-1]
    rw = jnp.pad(router_w, ((0, 0), (0, LANE - ne)))
    rb = jnp.pad(router_b, ((0, 0), (0, LANE - ne)), constant_values=-1e30)
    return pl.pallas_call(
        functools.partial(_out_ln_kernel, alpha=alpha),
        grid=(t // tm,),
        in_specs=[pl.BlockSpec((tm, d), row), pl.BlockSpec((tm, d), row), pl.BlockSpec((d, d), const),
                  pl.BlockSpec((1, d), const), pl.BlockSpec((1, d), const),
                  pl.BlockSpec((d, LANE), const), pl.BlockSpec((d, LANE), const),
                  pl.BlockSpec((1, LANE), const)],
        out_specs=[pl.BlockSpec((tm, d), row), pl.BlockSpec((tm, d), row), pl.BlockSpec((tm, LANE), row)],
        out_shape=[jax.ShapeDtypeStruct((t, d), F32), jax.ShapeDtypeStruct((t, d), BF16),
                   jax.ShapeDtypeStruct((t, LANE), F32)],
        compiler_params=_cparams("parallel"),
        name="out_ln",
    )(mixed, x, w_out, ln_g, ln_b, *_split_hi_lo(rw), rb)


def _route_kernel(lg_ref, idx_o, rank_o, gate_o, cnt_o, base_ref):
    i = pl.program_id(0)
    tm = lg_ref.shape[0]

    @pl.when(i == 0)
    def _():
        base_ref[...] = jnp.zeros_like(base_ref)

    l = lg_ref[...]
    lane = lax.broadcasted_iota(jnp.int32, l.shape, 1)
    sels, vals, idxs = [], [], []
    for _ in range(TOP_K):
        m = jnp.max(l, -1, keepdims=True)
        idx = jnp.min(jnp.where(l == m, lane, LANE), -1, keepdims=True)
        sel = lane == idx
        l = jnp.where(sel, -jnp.inf, l)
        sels.append(sel)
        vals.append(m)
        idxs.append(idx)
    ex = [jnp.exp(v - vals[0]) for v in vals]
    den = ex[0] + ex[1] + ex[2] + ex[3]
    onehot = jnp.zeros(l.shape, F32)
    for sel in sels:
        onehot = onehot + jnp.where(sel, 1.0, 0.0)
    rows = lax.broadcasted_iota(jnp.int32, (tm, tm), 0)
    cols = lax.broadcasted_iota(jnp.int32, (tm, tm), 1)
    before = jnp.where(rows > cols, 1.0, 0.0).astype(BF16)
    seen = _dot(before, onehot.astype(BF16)) + base_ref[...]
    idx_out = jnp.zeros(l.shape, jnp.int32)
    rank_out = jnp.zeros(l.shape, jnp.int32)
    gate_out = jnp.zeros(l.shape, F32)
    for k in range(TOP_K):
        rank = jnp.sum(jnp.where(sels[k], seen, 0.0), -1, keepdims=True).astype(jnp.int32)
        idx_out = jnp.where(lane == k, idxs[k], idx_out)
        rank_out = jnp.where(lane == k, rank, rank_out)
        gate_out = jnp.where(lane == k, ex[k] / den, gate_out)
    idx_o[...] = idx_out
    rank_o[...] = rank_out
    gate_o[...] = gate_out
    base_ref[...] = base_ref[...] + jnp.sum(onehot, 0, keepdims=True)
    cnt_o[...] = base_ref[...]


def _route(logits, n_experts, tm=512):
    t = logits.shape[0]
    row = lambda i: (i, 0)
    tile = pl.BlockSpec((tm, LANE), row)
    idx, rank, gates, counts = pl.pallas_call(
        _route_kernel,
        grid=(t // tm,),
        in_specs=[tile],
        out_specs=[tile, tile, tile, pl.BlockSpec((1, LANE), lambda i: (0, 0))],
        out_shape=[jax.ShapeDtypeStruct((t, LANE), jnp.int32), jax.ShapeDtypeStruct((t, LANE), jnp.int32),
                   jax.ShapeDtypeStruct((t, LANE), F32), jax.ShapeDtypeStruct((1, LANE), F32)],
        scratch_shapes=[pltpu.VMEM((1, LANE), F32)],
        compiler_params=_cparams("arbitrary"),
        name="route",
    )(logits)
    idx, rank, gates = idx[:, :TOP_K], rank[:, :TOP_K], gates[:, :TOP_K]
    counts = counts[0, :n_experts].astype(jnp.int32)
    n_blocks = -(-(t * TOP_K) // MOE_BLOCK) + n_experts
    padded = (counts + MOE_BLOCK - 1) // MOE_BLOCK * MOE_BLOCK
    pad_end = jnp.cumsum(padded)
    pad_start = pad_end - padded
    start_of = jnp.sum(jnp.where(idx[..., None] == jnp.arange(n_experts), pad_start, 0), -1)
    dest = (start_of + rank).astype(jnp.int32)
    tok = jnp.broadcast_to(jnp.arange(t, dtype=jnp.int32)[:, None], dest.shape)
    row_tok = jnp.zeros((n_blocks * MOE_BLOCK,), jnp.int32).at[dest.reshape(-1)].set(tok.reshape(-1))
    block_start = jnp.arange(n_blocks, dtype=jnp.int32) * MOE_BLOCK
    block_e = jnp.minimum(jnp.sum(pad_end[None, :] <= block_start[:, None], -1), n_experts - 1)
    n_used = (pad_end[-1:] // MOE_BLOCK).astype(jnp.int32)
    return gates, row_tok, dest, block_e.astype(jnp.int32), n_used


_W_SLAB = 512


def _expert_kernel(be_ref, nused_ref, x_ref, *refs, d_ff, n_gu, n_dn):
    wgu = refs[:n_gu]
    bgu_ref = refs[n_gu]
    wdn = refs[n_gu + 1:n_gu + 1 + n_dn]
    bd_ref = refs[n_gu + 1 + n_dn]
    o_ref = refs[n_gu + 2 + n_dn]
    i = pl.program_id(0)

    @pl.when(i < nused_ref[0])
    def _():
        gu = bgu_ref[...]
        for q, w in enumerate(wgu):
            gu = gu + _dot(x_ref[:, q * _W_SLAB:(q + 1) * _W_SLAB], w[...].astype(BF16))
        gate = jnp.minimum(gu[:, :d_ff], SWIGLU_LIMIT)
        up = jnp.clip(gu[:, d_ff:], -SWIGLU_LIMIT, SWIGLU_LIMIT)
        h = ((up + 1.0) * gate * jax.nn.sigmoid(SWIGLU_ALPHA * gate)).astype(BF16)
        out = bd_ref[...]
        for q, w in enumerate(wdn):
            out = out + _dot(h[:, q * _W_SLAB:(q + 1) * _W_SLAB], w[...].astype(BF16))
        o_ref[...] = out.astype(o_ref.dtype)

    @pl.when(i >= nused_ref[0])
    def _():
        o_ref[...] = jnp.zeros_like(o_ref)


def _experts(xs, block_e, n_used, w_gu, b_gu, w_down, b_down, layer):
    rows, d = xs.shape
    n_blocks = rows // MOE_BLOCK
    d_ff = w_down.shape[2]
    n_gu = d // _W_SLAB
    n_dn = d_ff // _W_SLAB
    xrow = lambda i, be, nu: (jnp.minimum(i, nu[0] - 1), 0)
    slab = lambda n, q: pl.BlockSpec((None, None, _W_SLAB, n), lambda i, be, nu: (layer, be[i], q, 0),
                                     pipeline_mode=pl.Buffered(1))
    bias = lambda n: pl.BlockSpec((None, None, 1, n), lambda i, be, nu: (layer, be[i], 0, 0))
    grid_spec = pltpu.PrefetchScalarGridSpec(
        num_scalar_prefetch=2,
        grid=(n_blocks,),
        in_specs=([pl.BlockSpec((MOE_BLOCK, d), xrow)]
                  + [slab(2 * d_ff, q) for q in range(n_gu)] + [bias(2 * d_ff)]
                  + [slab(d, q) for q in range(n_dn)] + [bias(d)]),
        out_specs=pl.BlockSpec((MOE_BLOCK, d), lambda i, be, nu: (i, 0)),
    )
    return pl.pallas_call(
        functools.partial(_expert_kernel, d_ff=d_ff, n_gu=n_gu, n_dn=n_dn),
        grid_spec=grid_spec,
        out_shape=jax.ShapeDtypeStruct((rows, d), BF16),
        compiler_params=_cparams("arbitrary"),
        name="experts",
    )(block_e, n_used, xs, *([w_gu] * n_gu), b_gu[:, :, None, :], *([w_down] * n_dn), b_down[:, :, None, :])


def _combine_ln_kernel(*refs, alpha):
    yk = refs[:TOP_K]
    gw_ref, x_ref, g_ref, b_ref, o_ref, ob_ref = refs[TOP_K:]
    acc = alpha * x_ref[...]
    for k in range(TOP_K):
        acc = acc + gw_ref[:, k:k + 1] * yk[k][...].astype(F32)
    y = _layer_norm(acc, g_ref[...], b_ref[...])
    o_ref[...] = y
    ob_ref[...] = y.astype(BF16)


def _combine_ln(yg, gw, x, ln_g, ln_b, alpha, tm=512):
    t, d = x.shape
    row = lambda i: (i, 0)
    const = lambda i: (0, 0)
    choice = lambda k: pl.BlockSpec((None, tm, d), lambda i: (k, i, 0))
    return pl.pallas_call(
        functools.partial(_combine_ln_kernel, alpha=alpha),
        grid=(t // tm,),
        in_specs=[choice(k) for k in range(TOP_K)] + [
            pl.BlockSpec((tm, TOP_K), row), pl.BlockSpec((tm, d), row),
            pl.BlockSpec((1, d), const), pl.BlockSpec((1, d), const)],
        out_specs=[pl.BlockSpec((tm, d), row), pl.BlockSpec((tm, d), row)],
        out_shape=[jax.ShapeDtypeStruct((t, d), F32), jax.ShapeDtypeStruct((t, d), BF16)],
        compiler_params=_cparams("parallel"),
        name="combine_ln",
    )(*([yg] * TOP_K), gw, x, ln_g, ln_b)


def kernel(x, w_in, w_in_vres, shift_mu, shift_mu_vres, rwkv_w0, rwkv_w2, rwkv_a0, rwkv_a2, rwkv_v0, rwkv_v2, rwkv_g2, rwkv_k_k, rwkv_k_a, rwkv_r_k, rwkv_gn_g, rwkv_gn_b, lru_conv_w, lru_conv_b, lru_wa, lru_ba, lru_wx, lru_bx, lru_lambda, conf_conv_w, conf_conv_b, conf_ln_g, conf_ln_b, w_branch, w_out, ln1_g, ln1_b, router_w, router_b, exp_w_gu, exp_b_gu, exp_w_down, exp_b_down, ln2_g, ln2_b):
    batch, seq, d = x.shape
    depth = w_in.shape[0]
    c = rwkv_w0.shape[-1]
    n_rwkv = shift_mu.shape[-1]
    n_experts = router_w.shape[-1]
    t = batch * seq
    alpha = float((2 * depth) ** 0.25)
    vec = lambda a: a.reshape(1, -1)
    pad_rows = lambda w, lo, n: jnp.pad(w, ((lo, n - lo - w.shape[0]), (0, 0)))

    xf = x.reshape(t, d)
    xb = xf.astype(BF16)
    v_first = None
    for l in range(depth):
        wl = w_in[l].astype(BF16)
        d_lora = rwkv_w2.shape[1]
        d_vres = w_in_vres.shape[-1]
        w_rwkv = jnp.pad(wl[:, :n_rwkv], ((0, 0), (0, _C_VRES - n_rwkv)))
        mu = jnp.pad(shift_mu[l], (0, _C_VRES - n_rwkv))
        if l == 0:
            w_rwkv = jnp.pad(w_rwkv, ((0, 0), (0, RWKV_COLS - _C_VRES)))
            mu = jnp.pad(mu, (0, RWKV_COLS - _C_VRES))
        else:
            w_rwkv = jnp.concatenate(
                [w_rwkv, jnp.pad(w_in_vres[l - 1].astype(BF16), ((0, 0), (0, LANE - d_vres)))], axis=1)
            mu = jnp.concatenate([mu, jnp.pad(shift_mu_vres[l - 1], (0, LANE - d_vres))])
        o = n_rwkv
        p_rwkv = _matmul(xb, w_rwkv, F32, 1024, RWKV_COLS // 4)
        p_lru = _matmul(xb, wl[:, o:o + 2 * c], F32, 1024, 1024)
        p_conf = _matmul(xb, wl[:, o + 2 * c:o + 4 * c], F32, 1024, 1024)
        gates = _matmul(xb, wl[:, o + 4 * c:], BF16, 1024, 1024, act="sigmoid")

        w2 = pad_rows(rwkv_w2[l], 0, LANE)
        a2 = pad_rows(rwkv_a2[l], d_lora, LANE)
        g2 = pad_rows(rwkv_g2[l], 0, 2 * LANE)
        vres = None
        if l > 0:
            vres = (vec(rwkv_v0[l - 1]), pad_rows(rwkv_v2[l - 1], 0, LANE), v_first)
        v_t, *chunk_ops = _rwkv_prep(
            p_rwkv, vec(mu), vec(rwkv_w0[l]), vec(rwkv_a0[l]), w2, a2, g2,
            vec(rwkv_k_k[l]), vec(rwkv_k_a[l]), vec(rwkv_r_k[l]), vres, batch, seq)
        if l == 0:
            v_first = v_t
        y_a = _rwkv_chunk(chunk_ops, rwkv_gn_g[l], rwkv_gn_b[l])

        y_b = _lru(p_lru, lru_conv_w[l], vec(lru_conv_b[l]), lru_wa[l], vec(lru_ba[l]),
                   lru_wx[l], vec(lru_bx[l]), vec(lru_lambda[l]), batch, seq)
        y_c = _conformer(p_conf, conf_conv_w[l], vec(conf_conv_b[l]), vec(conf_ln_g[l]),
                         vec(conf_ln_b[l]), batch, seq)

        mixed = _merge(y_a, y_b, y_c, w_branch[l].astype(BF16), gates)
        x1, x1b, logits = _out_ln(mixed, xf, w_out[l].astype(BF16), vec(ln1_g[l]), vec(ln1_b[l]),
                                  router_w[l], vec(router_b[l]), alpha)

        gw, row_tok, dest, block_e, n_used = _route(logits, n_experts)
        xs = x1b.at[row_tok].get(mode="promise_in_bounds")
        ys = _experts(xs, block_e, n_used, exp_w_gu, exp_b_gu, exp_w_down, exp_b_down, l)
        yg = ys.at[dest.T.reshape(-1)].get(mode="promise_in_bounds").reshape(TOP_K, t, d)
        xf, xb = _combine_ln(yg, gw, x1, vec(ln2_g[l]), vec(ln2_b[l]), alpha)
    return xf.reshape(batch, seq, d)
```

```python
import functools

import jax
import jax.numpy as jnp
from jax import lax
from jax.experimental import pallas as pl
from jax.experimental.pallas import tpu as pltpu

F32 = jnp.float32
BF16 = jnp.bfloat16

HEAD_DIM = 64
RWKV_CHUNK = 64
RWKV_GN_EPS = 64e-5
LRU_C = 8.0
LN_EPS = 1e-5
TOP_K = 4
MOE_BLOCK = 512
SWIGLU_ALPHA = 1.702
SWIGLU_LIMIT = 7.0
LANE = 128
SUBLANES = 8
VMEM_LIMIT = 56 * 1024 * 1024


def _cparams(*sem):
    return pltpu.CompilerParams(dimension_semantics=sem, vmem_limit_bytes=VMEM_LIMIT)


def _dot(a, b):
    return jnp.dot(a, b, preferred_element_type=F32)


def _dot_nt(a, b):
    return lax.dot_general(a, b, (((1,), (1,)), ((), ())), preferred_element_type=F32)


def _dot_tn(a, b):
    return lax.dot_general(a, b, (((0,), (0,)), ((), ())), preferred_element_type=F32)


def _split_hi_lo(w):
    hi = w.astype(BF16)
    lo = (w - hi.astype(F32)).astype(BF16)
    return hi, lo


def _dot3(a, b_hi, b_lo):
    a_hi, a_lo = _split_hi_lo(a)
    return _dot(a_hi, b_hi) + (_dot(a_lo, b_hi) + _dot(a_hi, b_lo))


def _softplus(z):
    return jnp.maximum(z, 0.0) + jnp.log(1.0 + jnp.exp(-jnp.abs(z)))


def _layer_norm(v, g, b):
    mu = jnp.mean(v, -1, keepdims=True)
    c = v - mu
    var = jnp.mean(c * c, -1, keepdims=True)
    return c * lax.rsqrt(var + LN_EPS) * g + b


def _mm_kernel(x_ref, w_ref, o_ref, *, act):
    acc = _dot(x_ref[...], w_ref[...])
    if act == "sigmoid":
        acc = jax.nn.sigmoid(acc)
    o_ref[...] = acc.astype(o_ref.dtype)


def _matmul(x, w, out_dtype, tm, tn, act=None):
    m, k = x.shape
    n = w.shape[1]
    return pl.pallas_call(
        functools.partial(_mm_kernel, act=act),
        grid=(n // tn, m // tm),
        in_specs=[pl.BlockSpec((tm, k), lambda j, i: (i, 0)),
                  pl.BlockSpec((k, tn), lambda j, i: (0, j))],
        out_specs=pl.BlockSpec((tm, tn), lambda j, i: (i, j)),
        out_shape=jax.ShapeDtypeStruct((m, n), out_dtype),
        compiler_params=_cparams("parallel", "parallel"),
        name="in_proj",
    )(x, w)


RWKV_COLS = 3584
_C_LORA = 3072
_C_GATE = 3200
_C_VRES = 3456
_N_PREP_HM = 10


def _rwkv_prep_kernel(*refs, tiles_per_batch, has_vres, c_mix):
    n_in = 18 if has_vres else 14
    (p_ref, pprev_ref, mu_ref, w0_ref, a0_ref, w2h, w2l, a2h, a2l, g2_ref,
     kkw_ref, kaw_ref, rkw_ref, seg_ref) = refs[:14]
    if has_vres:
        v0_ref, v2h, v2l, vfirst_ref = refs[14:18]
    v_o = refs[n_in]
    at_o, bt_o, kt_o, rt_o, bp_o, kp_o, vh_o, ecl_o, bonus_o, g_o = refs[n_in + 1:]
    i = pl.program_id(0)
    tm = p_ref.shape[0]
    first = (i % tiles_per_batch) == 0

    def shifted(lo, hi):
        p = p_ref[:, lo:hi]
        prev_row = jnp.where(first, 0.0, pprev_ref[7:8, lo:hi])
        rows = lax.broadcasted_iota(jnp.int32, p.shape, 0)
        prev = jnp.where(rows == 0, prev_row, pltpu.roll(p, 1, axis=0))
        return p + (prev - p) * mu_ref[:, lo:hi]

    def put(o_ref, val):
        for h in range(c_mix // HEAD_DIM):
            o_ref[0, h] = val[:, h * HEAD_DIM:(h + 1) * HEAD_DIM].astype(o_ref.dtype)

    def head_sum(val):
        hi, lo = _split_hi_lo(val)
        seg = seg_ref[...]
        parts = []
        for j in range(c_mix // LANE):
            sl = slice(j * LANE, (j + 1) * LANE)
            parts.append(_dot(hi[:, sl], seg) + _dot(lo[:, sl], seg))
        return jnp.concatenate(parts, axis=1)

    r = shifted(0, c_mix)
    k_raw = shifted(c_mix, 2 * c_mix)
    v = shifted(2 * c_mix, 3 * c_mix)
    lora = shifted(_C_LORA, _C_LORA + LANE)
    w_log = -_softplus(-(w0_ref[...] + _dot3(jnp.tanh(lora), w2h[...], w2l[...]))) - 0.5
    lw = -jnp.exp(w_log)
    alr = jax.nn.sigmoid(a0_ref[...] + _dot3(lora, a2h[...], a2l[...]))
    gate_in = jax.nn.sigmoid(shifted(_C_GATE, _C_GATE + 2 * LANE))
    put(g_o, _dot(gate_in.astype(BF16), g2_ref[...]))
    if has_vres:
        pv = shifted(_C_VRES, _C_VRES + LANE)
        mix = jax.nn.sigmoid(v0_ref[...] + _dot3(pv, v2h[...], v2l[...]))
        v = v + (vfirst_ref[...] - v) * mix
    v_o[...] = v
    put(vh_o, v)

    kk = k_raw * kkw_ref[...]
    kk = kk / jnp.maximum(jnp.sqrt(jnp.maximum(head_sum(kk * kk), 0.0)), 1e-12)
    k = k_raw * (1.0 + (alr - 1.0) * kaw_ref[...])
    put(bonus_o, head_sum(r * k * rkw_ref[...]) * v)

    rows = lax.broadcasted_iota(jnp.int32, (tm, tm), 0)
    cols = lax.broadcasted_iota(jnp.int32, (tm, tm), 1)
    same = (rows // RWKV_CHUNK) == (cols // RWKV_CHUNK)
    tril = jnp.where(same & (rows >= cols), 1.0, 0.0).astype(BF16)
    ones = jnp.where(same, 1.0, 0.0).astype(BF16)
    l0 = lw.astype(BF16)
    rem = lw - l0.astype(F32)
    l1 = rem.astype(BF16)
    l2 = (rem - l1.astype(F32)).astype(BF16)
    cum = _dot(tril, l0) + (_dot(tril, l1) + _dot(tril, l2))
    tot = _dot(ones, l0) + (_dot(ones, l1) + _dot(ones, l2))

    e_neg = jnp.exp(-cum)
    e_tail = jnp.exp(tot - cum)
    b = kk * alr
    put(at_o, -kk * jnp.exp(cum - lw))
    put(bt_o, b * e_neg)
    put(kt_o, k * e_neg)
    put(rt_o, r * jnp.exp(cum))
    put(bp_o, b * e_tail)
    put(kp_o, k * e_tail)
    put(ecl_o, jnp.exp(tot))


def _rwkv_prep(p, mu, w0, a0, w2, a2, g2, k_k, k_a, r_k, vres, batch, seq, tm=256):
    t = p.shape[0]
    c_mix = w0.shape[-1]
    n_heads = c_mix // HEAD_DIM
    has_vres = vres is not None
    tiles_per_batch = seq // tm
    row = lambda i: (i, 0)
    const = lambda i: (0, 0)
    hm = lambda i: (i // tiles_per_batch, 0, i % tiles_per_batch, 0)
    vec = pl.BlockSpec((1, c_mix), const)
    lora_w = pl.BlockSpec((LANE, c_mix), const)
    lane_id = jnp.arange(LANE) // HEAD_DIM
    seg = (lane_id[:, None] == lane_id[None, :]).astype(BF16)
    in_specs = [
        pl.BlockSpec((tm, RWKV_COLS), row),
        pl.BlockSpec((8, RWKV_COLS), lambda i: (jnp.maximum(i * (tm // 8) - 1, 0), 0)),
        pl.BlockSpec((1, RWKV_COLS), const),
        vec, vec, lora_w, lora_w, lora_w, lora_w,
        pl.BlockSpec((2 * LANE, c_mix), const),
        vec, vec, vec, pl.BlockSpec((LANE, LANE), const),
    ]
    args = [p, p, mu, w0, a0, *_split_hi_lo(w2), *_split_hi_lo(a2), g2.astype(BF16), k_k, k_a, r_k, seg]
    if has_vres:
        v0, v2, v_first = vres
        in_specs += [vec, lora_w, lora_w, pl.BlockSpec((tm, c_mix), row)]
        args += [v0, *_split_hi_lo(v2), v_first]
    hm_spec = pl.BlockSpec((1, n_heads, tm, HEAD_DIM), hm)
    hm_shape = lambda dt: jax.ShapeDtypeStruct((batch, n_heads, seq, HEAD_DIM), dt)
    out_shape = [jax.ShapeDtypeStruct((t, c_mix), F32)] + [hm_shape(BF16)] * 7 + [hm_shape(F32)] * 3
    out_specs = [pl.BlockSpec((tm, c_mix), row)] + [hm_spec] * _N_PREP_HM
    return pl.pallas_call(
        functools.partial(_rwkv_prep_kernel, tiles_per_batch=tiles_per_batch,
                          has_vres=has_vres, c_mix=c_mix),
        grid=(t // tm,), in_specs=in_specs, out_specs=out_specs, out_shape=out_shape,
        compiler_params=_cparams("parallel"), name="rwkv_prep",
    )(*args)


def _rwkv_chunk_kernel(at_ref, bt_ref, kt_ref, rt_ref, bp_ref, kp_ref, v_ref, ecl_ref, bonus_ref,
                       g_ref, gng_ref, gnb_ref, y_ref, st_ref, *, heads):
    c = pl.program_id(2)
    L = RWKV_CHUNK
    n = HEAD_DIM

    @pl.when(c == 0)
    def _():
        st_ref[...] = jnp.zeros_like(st_ref)

    ti = lax.broadcasted_iota(jnp.int32, (L, L), 0)
    si = lax.broadcasted_iota(jnp.int32, (L, L), 1)
    strict = ti > si
    incl = ti >= si
    eye = ti == si
    eye_f = jnp.where(eye, 1.0, 0.0)
    bf = lambda t: t.astype(BF16)

    hs = range(heads)
    each = lambda f, *cols: [f(*xs) for xs in zip(*cols)]
    at = [at_ref[0, h] for h in hs]
    bt = [bt_ref[0, h] for h in hs]
    kt = [kt_ref[0, h] for h in hs]
    rt = [rt_ref[0, h] for h in hs]
    vb = [v_ref[0, h] for h in hs]
    bp = [bp_ref[0, h] for h in hs]
    kp = [kp_ref[0, h] for h in hs]
    ar = each(lambda a, r: jnp.concatenate([a, r], axis=0), at, rt)
    xb = each(_dot_nt, ar, bt)
    xk = each(_dot_nt, ar, kt)
    a_ab = [jnp.where(strict, x[:L], 0.0) for x in xb]
    a_ak = [bf(jnp.where(strict, x[:L], 0.0)) for x in xk]
    m_rb = [bf(jnp.where(incl, x[L:], 0.0)) for x in xb]
    m_rk = [bf(jnp.where(incl, x[L:], 0.0)) for x in xk]
    pw = a_ab
    tinv = [eye_f + a for a in a_ab]
    for _ in range(5):
        pwb = [bf(p) for p in pw]
        pw = each(_dot, pwb, pwb)
        tinv = each(lambda t, p: t + _dot(bf(t), bf(p)), tinv, pw)
    tb = [bf(t) for t in tinv]
    u = each(lambda a, v: bf(_dot(a, v)), a_ak, vb)
    e1 = each(lambda t, x: bf(_dot(t, x)), tb, u)
    ah = each(lambda t, a: bf(_dot(t, a)), tb, at)
    rh = each(lambda r, m, a: bf(r.astype(F32) + _dot(m, a)), rt, m_rb, ah)
    o1 = each(lambda mb, e, mk, v: _dot(mb, e) + _dot(mk, v), m_rb, e1, m_rk, vb)
    phi = [bf(jnp.where(eye, ecl_ref[0, h, 0:1, :], 0.0) + _dot_tn(bp[h], ah[h])) for h in hs]
    psi = each(lambda b, e, k, v: _dot_tn(b, e) + _dot_tn(k, v), bp, e1, kp, vb)
    stb = [bf(st_ref[h]) for h in hs]
    ys = each(lambda r, p, s: _dot(jnp.concatenate([r, p], axis=0), s), rh, phi, stb)
    for h in hs:
        st_ref[h] = ys[h][L:] + psi[h]
    outs = []
    for h in hs:
        y = ys[h][:L] + o1[h]
        mu = jnp.mean(y, -1, keepdims=True)
        yc = y - mu
        var = jnp.mean(yc * yc, -1, keepdims=True)
        yn = yc * lax.rsqrt(var + RWKV_GN_EPS) * gng_ref[h] + gnb_ref[h]
        outs.append((yn + bonus_ref[0, h]) * g_ref[0, h])
    y_ref[...] = jnp.concatenate(outs, axis=1).astype(y_ref.dtype)


def _rwkv_chunk(ops, gn_g, gn_b, heads_per_step=16):
    b, h, s, n = ops[0].shape
    hb = heads_per_step
    n_chunks = s // RWKV_CHUNK
    seq_spec = pl.BlockSpec((1, hb, RWKV_CHUNK, n), lambda bi, hi, ci: (bi, hi, ci, 0))
    par_spec = pl.BlockSpec((hb, 1, n), lambda bi, hi, ci: (hi, 0, 0))
    par = lambda w: w.reshape(h, 1, n)
    return pl.pallas_call(
        functools.partial(_rwkv_chunk_kernel, heads=hb),
        grid=(b, h // hb, n_chunks),
        in_specs=[seq_spec] * _N_PREP_HM + [par_spec] * 2,
        out_specs=pl.BlockSpec((RWKV_CHUNK, hb * n), lambda bi, hi, ci: (bi * n_chunks + ci, hi)),
        out_shape=jax.ShapeDtypeStruct((b * s, h * n), BF16),
        scratch_shapes=[pltpu.VMEM((hb, n, n), F32)],
        compiler_params=_cparams("parallel", "parallel", "arbitrary"),
        name="rwkv_chunk",
    )(*ops, par(gn_g), par(gn_b))


def _lru_kernel(gate_ref, x_ref, cw_ref, cb_ref, wa_ref, ba_ref, wx_ref, bx_ref, lam_ref,
                y_ref, ext_ref, h_ref, *, conv_w):
    ti = pl.program_id(1)
    tm = x_ref.shape[0]

    @pl.when(ti == 0)
    def _():
        ext_ref[0:8, :] = jnp.zeros((8, ext_ref.shape[1]), F32)
        h_ref[...] = jnp.zeros_like(h_ref)

    x = x_ref[...]
    ext_ref[8:8 + tm, :] = x
    xc = cb_ref[...] + cw_ref[conv_w - 1:conv_w, :] * x
    for d in range(1, conv_w):
        xc = xc + cw_ref[conv_w - 1 - d:conv_w - d, :] * ext_ref[8 - d:8 - d + tm, :]
    ext_ref[0:8, :] = x[tm - 8:tm, :]
    xb = xc.astype(BF16)

    def gate(w_ref, b_ref):
        parts = [_dot(xb[:, j * LANE:(j + 1) * LANE], w_ref[j]) for j in range(w_ref.shape[0])]
        return jax.nn.sigmoid(jnp.concatenate(parts, axis=1) + b_ref[...])

    rg = gate(wa_ref, ba_ref)
    ig = gate(wx_ref, bx_ref)
    log_a = -LRU_C * rg * _softplus(-lam_ref[...])
    a = jnp.exp(log_a)
    th = jnp.tanh(log_a)
    b = jnp.sqrt(-2.0 * th / (1.0 - th)) * (ig * xc)
    in_group = lax.broadcasted_iota(jnp.int32, a.shape, 0) % SUBLANES
    d = 1
    while d < SUBLANES:
        keep = in_group >= d
        b = b + a * jnp.where(keep, pltpu.roll(b, d, axis=0), 0.0)
        a = a * jnp.where(keep, pltpu.roll(a, d, axis=0), 1.0)
        d *= 2
    carry = h_ref[...]
    groups = []
    for g in range(tm // SUBLANES):
        rows = slice(g * SUBLANES, (g + 1) * SUBLANES)
        hg = a[rows] * carry + b[rows]
        carry = hg[SUBLANES - 1:SUBLANES, :]
        groups.append(hg)
    h_ref[...] = carry
    h = jnp.concatenate(groups, axis=0)
    y_ref[...] = (h * jax.nn.gelu(gate_ref[...], approximate=True)).astype(y_ref.dtype)


def _pair_blocks(w):
    h, n, _ = w.shape
    z = jnp.zeros((h // 2, n, n), w.dtype)
    top = jnp.concatenate([w[0::2], z], axis=2)
    bot = jnp.concatenate([z, w[1::2]], axis=2)
    return jnp.concatenate([top, bot], axis=1).astype(BF16)


def _lru(proj, conv_w, conv_b, wa, ba, wx, bx, lam, batch, seq, tm=256):
    c = conv_b.shape[-1]
    tiles = seq // tm
    row0 = lambda b, i: (b * tiles + i, 0)
    row1 = lambda b, i: (b * tiles + i, 1)
    const = lambda b, i: (0, 0)
    vec = pl.BlockSpec((1, c), const)
    width = conv_w.shape[0]
    blk = pl.BlockSpec((c // LANE, LANE, LANE), lambda b, i: (0, 0, 0))
    return pl.pallas_call(
        functools.partial(_lru_kernel, conv_w=width),
        grid=(batch, tiles),
        in_specs=[pl.BlockSpec((tm, c), row0), pl.BlockSpec((tm, c), row1),
                  pl.BlockSpec((width, c), const), vec, blk, vec, blk, vec, vec],
        out_specs=pl.BlockSpec((tm, c), row0),
        out_shape=jax.ShapeDtypeStruct((batch * seq, c), BF16),
        scratch_shapes=[pltpu.VMEM((tm + 8, c), F32), pltpu.VMEM((1, c), F32)],
        compiler_params=_cparams("parallel", "arbitrary"),
        name="rglru",
    )(proj, proj, conv_w, conv_b, _pair_blocks(wa), ba, _pair_blocks(wx), bx, lam)


_CONF_HALO = 32


def _conf_kernel(val_ref, gate_ref, cw_ref, cb_ref, g_ref, b_ref, y_ref, ext_ref, sh_ref, *, conv_w):
    ti = pl.program_id(1)
    tm = val_ref.shape[0]

    @pl.when(ti == 0)
    def _():
        ext_ref[0:_CONF_HALO, :] = jnp.zeros((_CONF_HALO, ext_ref.shape[1]), F32)

    u = val_ref[...] * jax.nn.sigmoid(gate_ref[...])
    ext_ref[_CONF_HALO:_CONF_HALO + tm, :] = u
    ext = ext_ref[...]
    for r in range(1, SUBLANES):
        sh_ref[r - 1] = pltpu.roll(ext, r, axis=0)
    acc = cb_ref[...] + cw_ref[conv_w - 1:conv_w, :] * u
    for d in range(1, conv_w):
        q, r = divmod(d, SUBLANES)
        lo = _CONF_HALO - q * SUBLANES
        tap = ext_ref[lo:lo + tm, :] if r == 0 else sh_ref[r - 1, lo:lo + tm, :]
        acc = acc + cw_ref[conv_w - 1 - d:conv_w - d, :] * tap
    ext_ref[0:_CONF_HALO, :] = u[tm - _CONF_HALO:tm, :]
    y = _layer_norm(acc, g_ref[...], b_ref[...])
    y_ref[...] = (y * jax.nn.sigmoid(y)).astype(y_ref.dtype)


def _conformer(proj, conv_w, conv_b, ln_g, ln_b, batch, seq, tm=256):
    c = conv_b.shape[-1]
    tiles = seq // tm
    row0 = lambda b, i: (b * tiles + i, 0)
    row1 = lambda b, i: (b * tiles + i, 1)
    const = lambda b, i: (0, 0)
    vec = pl.BlockSpec((1, c), const)
    width = conv_w.shape[0]
    return pl.pallas_call(
        functools.partial(_conf_kernel, conv_w=width),
        grid=(batch, tiles),
        in_specs=[pl.BlockSpec((tm, c), row0), pl.BlockSpec((tm, c), row1),
                  pl.BlockSpec((width, c), const), vec, vec, vec],
        out_specs=pl.BlockSpec((tm, c), row0),
        out_shape=jax.ShapeDtypeStruct((batch * seq, c), BF16),
        scratch_shapes=[pltpu.VMEM((tm + _CONF_HALO, c), F32),
                        pltpu.VMEM((SUBLANES - 1, tm + _CONF_HALO, c), F32)],
        compiler_params=_cparams("parallel", "arbitrary"),
        name="conformer",
    )(proj, proj, conv_w, conv_b, ln_g, ln_b)


def _merge_kernel(ya_ref, yb_ref, yc_ref, wb_ref, ga_ref, gb_ref, gc_ref, o_ref):
    acc = ga_ref[...].astype(F32) * _dot(ya_ref[...], wb_ref[0])
    acc = acc + gb_ref[...].astype(F32) * _dot(yb_ref[...], wb_ref[1])
    acc = acc + gc_ref[...].astype(F32) * _dot(yc_ref[...], wb_ref[2])
    o_ref[...] = acc.astype(o_ref.dtype)


def _merge(ya, yb, yc, w_branch, gates, tm=1024, tn=1024):
    t, c = ya.shape
    d = w_branch.shape[-1]
    nj = d // tn
    ysp = pl.BlockSpec((tm, c), lambda j, i: (i, 0))
    gsp = lambda k: pl.BlockSpec((tm, tn), lambda j, i: (i, k * nj + j))
    return pl.pallas_call(
        _merge_kernel,
        grid=(nj, t // tm),
        in_specs=[ysp, ysp, ysp, pl.BlockSpec((3, c, tn), lambda j, i: (0, 0, j)),
                  gsp(0), gsp(1), gsp(2)],
        out_specs=pl.BlockSpec((tm, tn), lambda j, i: (i, j)),
        out_shape=jax.ShapeDtypeStruct((t, d), BF16),
        compiler_params=_cparams("parallel", "parallel"),
        name="merge",
    )(ya, yb, yc, w_branch, gates, gates, gates)


def _out_ln_kernel(m_ref, x_ref, w_ref, g_ref, b_ref, rwh_ref, rwl_ref, rb_ref,
                   o_ref, ob_ref, lg_ref, *, alpha):
    y = alpha * x_ref[...] + _dot(m_ref[...], w_ref[...])
    y = _layer_norm(y, g_ref[...], b_ref[...])
    o_ref[...] = y
    ob_ref[...] = y.astype(BF16)
    lg_ref[...] = _dot3(y, rwh_ref[...], rwl_ref[...]) + rb_ref[...]


def _out_ln(mixed, x, w_out, ln_g, ln_b, router_w, router_b, alpha, tm=512):
    t, d = x.shape
    row = lambda i: (i, 0)
    const = lambda i: (0, 0)
    ne = router_w.shape[-1]
    rw = jnp.pad(router_w, ((0, 0), (0, LANE - ne)))
    rb = jnp.pad(router_b, ((0, 0), (0, LANE - ne)), constant_values=-1e30)
    return pl.pallas_call(
        functools.partial(_out_ln_kernel, alpha=alpha),
        grid=(t // tm,),
        in_specs=[pl.BlockSpec((tm, d), row), pl.BlockSpec((tm, d), row), pl.BlockSpec((d, d), const),
                  pl.BlockSpec((1, d), const), pl.BlockSpec((1, d), const),
                  pl.BlockSpec((d, LANE), const), pl.BlockSpec((d, LANE), const),
                  pl.BlockSpec((1, LANE), const)],
        out_specs=[pl.BlockSpec((tm, d), row), pl.BlockSpec((tm, d), row), pl.BlockSpec((tm, LANE), row)],
        out_shape=[jax.ShapeDtypeStruct((t, d), F32), jax.ShapeDtypeStruct((t, d), BF16),
                   jax.ShapeDtypeStruct((t, LANE), F32)],
        compiler_params=_cparams("parallel"),
        name="out_ln",
    )(mixed, x, w_out, ln_g, ln_b, *_split_hi_lo(rw), rb)


def _route_kernel(lg_ref, idx_o, rank_o, gate_o, cnt_o, base_ref):
    i = pl.program_id(0)
    tm = lg_ref.shape[0]

    @pl.when(i == 0)
    def _():
        base_ref[...] = jnp.zeros_like(base_ref)

    l = lg_ref[...]
    lane = lax.broadcasted_iota(jnp.int32, l.shape, 1)
    sels, vals, idxs = [], [], []
    for _ in range(TOP_K):
        m = jnp.max(l, -1, keepdims=True)
        idx = jnp.min(jnp.where(l == m, lane, LANE), -1, keepdims=True)
        sel = lane == idx
        l = jnp.where(sel, -jnp.inf, l)
        sels.append(sel)
        vals.append(m)
        idxs.append(idx)
    ex = [jnp.exp(v - vals[0]) for v in vals]
    den = ex[0] + ex[1] + ex[2] + ex[3]
    onehot = jnp.zeros(l.shape, F32)
    for sel in sels:
        onehot = onehot + jnp.where(sel, 1.0, 0.0)
    rows = lax.broadcasted_iota(jnp.int32, (tm, tm), 0)
    cols = lax.broadcasted_iota(jnp.int32, (tm, tm), 1)
    before = jnp.where(rows > cols, 1.0, 0.0).astype(BF16)
    seen = _dot(before, onehot.astype(BF16)) + base_ref[...]
    idx_out = jnp.zeros(l.shape, jnp.int32)
    rank_out = jnp.zeros(l.shape, jnp.int32)
    gate_out = jnp.zeros(l.shape, F32)
    for k in range(TOP_K):
        rank = jnp.sum(jnp.where(sels[k], seen, 0.0), -1, keepdims=True).astype(jnp.int32)
        idx_out = jnp.where(lane == k, idxs[k], idx_out)
        rank_out = jnp.where(lane == k, rank, rank_out)
        gate_out = jnp.where(lane == k, ex[k] / den, gate_out)
    idx_o[...] = idx_out
    rank_o[...] = rank_out
    gate_o[...] = gate_out
    base_ref[...] = base_ref[...] + jnp.sum(onehot, 0, keepdims=True)
    cnt_o[...] = base_ref[...]


def _route(logits, n_experts, tm=512):
    t = logits.shape[0]
    row = lambda i: (i, 0)
    tile = pl.BlockSpec((tm, LANE), row)
    idx, rank, gates, counts = pl.pallas_call(
        _route_kernel,
        grid=(t // tm,),
        in_specs=[tile],
        out_specs=[tile, tile, tile, pl.BlockSpec((1, LANE), lambda i: (0, 0))],
        out_shape=[jax.ShapeDtypeStruct((t, LANE), jnp.int32), jax.ShapeDtypeStruct((t, LANE), jnp.int32),
                   jax.ShapeDtypeStruct((t, LANE), F32), jax.ShapeDtypeStruct((1, LANE), F32)],
        scratch_shapes=[pltpu.VMEM((1, LANE), F32)],
        compiler_params=_cparams("arbitrary"),
        name="route",
    )(logits)
    idx, rank, gates = idx[:, :TOP_K], rank[:, :TOP_K], gates[:, :TOP_K]
    counts = counts[0, :n_experts].astype(jnp.int32)
    n_blocks = -(-(t * TOP_K) // MOE_BLOCK) + n_experts
    padded = (counts + MOE_BLOCK - 1) // MOE_BLOCK * MOE_BLOCK
    pad_end = jnp.cumsum(padded)
    pad_start = pad_end - padded
    start_of = jnp.sum(jnp.where(idx[..., None] == jnp.arange(n_experts), pad_start, 0), -1)
    dest = (start_of + rank).astype(jnp.int32)
    tok = jnp.broadcast_to(jnp.arange(t, dtype=jnp.int32)[:, None], dest.shape)
    n_rows = n_blocks * MOE_BLOCK
    row_tok = (jnp.arange(n_rows, dtype=jnp.int32) % t).at[dest.reshape(-1)].set(tok.reshape(-1))
    block_start = jnp.arange(n_blocks, dtype=jnp.int32) * MOE_BLOCK
    block_e = jnp.minimum(jnp.sum(pad_end[None, :] <= block_start[:, None], -1), n_experts - 1)
    n_used = (pad_end[-1:] // MOE_BLOCK).astype(jnp.int32)
    return gates, row_tok, dest, block_e.astype(jnp.int32), n_used


_W_SLAB = 512


def _expert_kernel(be_ref, nused_ref, x_ref, *refs, d_ff, n_gu, n_dn):
    wgu = refs[:n_gu]
    bgu_ref = refs[n_gu]
    wdn = refs[n_gu + 1:n_gu + 1 + n_dn]
    bd_ref = refs[n_gu + 1 + n_dn]
    o_ref = refs[n_gu + 2 + n_dn]
    i = pl.program_id(0)

    @pl.when(i < nused_ref[0])
    def _():
        gu = bgu_ref[...]
        for q, w in enumerate(wgu):
            gu = gu + _dot(x_ref[:, q * _W_SLAB:(q + 1) * _W_SLAB], w[...].astype(BF16))
        gate = jnp.minimum(gu[:, :d_ff], SWIGLU_LIMIT)
        up = jnp.clip(gu[:, d_ff:], -SWIGLU_LIMIT, SWIGLU_LIMIT)
        h = ((up + 1.0) * gate * jax.nn.sigmoid(SWIGLU_ALPHA * gate)).astype(BF16)
        out = bd_ref[...]
        for q, w in enumerate(wdn):
            out = out + _dot(h[:, q * _W_SLAB:(q + 1) * _W_SLAB], w[...].astype(BF16))
        o_ref[...] = out.astype(o_ref.dtype)

    @pl.when(i >= nused_ref[0])
    def _():
        o_ref[...] = jnp.zeros_like(o_ref)


def _experts(xs, block_e, n_used, w_gu, b_gu, w_down, b_down, layer):
    rows, d = xs.shape
    n_blocks = rows // MOE_BLOCK
    d_ff = w_down.shape[2]
    n_gu = d // _W_SLAB
    n_dn = d_ff // _W_SLAB
    xrow = lambda i, be, nu: (jnp.minimum(i, nu[0] - 1), 0)
    slab = lambda n, q, bufs: pl.BlockSpec((None, None, _W_SLAB, n), lambda i, be, nu: (layer, be[i], q, 0),
                                           pipeline_mode=pl.Buffered(bufs))
    bias = lambda n: pl.BlockSpec((None, None, 1, n), lambda i, be, nu: (layer, be[i], 0, 0))
    grid_spec = pltpu.PrefetchScalarGridSpec(
        num_scalar_prefetch=2,
        grid=(n_blocks,),
        in_specs=([pl.BlockSpec((MOE_BLOCK, d), xrow)]
                  + [slab(2 * d_ff, q, 1) for q in range(n_gu)] + [bias(2 * d_ff)]
                  + [slab(d, q, 2) for q in range(n_dn)] + [bias(d)]),
        out_specs=pl.BlockSpec((MOE_BLOCK, d), lambda i, be, nu: (i, 0)),
    )
    return pl.pallas_call(
        functools.partial(_expert_kernel, d_ff=d_ff, n_gu=n_gu, n_dn=n_dn),
        grid_spec=grid_spec,
        out_shape=jax.ShapeDtypeStruct((rows, d), BF16),
        compiler_params=_cparams("arbitrary"),
        name="experts",
    )(block_e, n_used, xs, *([w_gu] * n_gu), b_gu[:, :, None, :], *([w_down] * n_dn), b_down[:, :, None, :])


def _combine_ln_kernel(*refs, alpha):
    yk = refs[:TOP_K]
    gw_ref, x_ref, g_ref, b_ref, o_ref, ob_ref = refs[TOP_K:]
    acc = alpha * x_ref[...]
    for k in range(TOP_K):
        acc = acc + gw_ref[:, k:k + 1] * yk[k][...].astype(F32)
    y = _layer_norm(acc, g_ref[...], b_ref[...])
    o_ref[...] = y
    ob_ref[...] = y.astype(BF16)


def _combine_ln(yg, gw, x, ln_g, ln_b, alpha, tm=512):
    t, d = x.shape
    row = lambda i: (i, 0)
    const = lambda i: (0, 0)
    choice = lambda k: pl.BlockSpec((None, tm, d), lambda i: (k, i, 0))
    return pl.pallas_call(
        functools.partial(_combine_ln_kernel, alpha=alpha),
        grid=(t // tm,),
        in_specs=[choice(k) for k in range(TOP_K)] + [
            pl.BlockSpec((tm, TOP_K), row), pl.BlockSpec((tm, d), row),
            pl.BlockSpec((1, d), const), pl.BlockSpec((1, d), const)],
        out_specs=[pl.BlockSpec((tm, d), row), pl.BlockSpec((tm, d), row)],
        out_shape=[jax.ShapeDtypeStruct((t, d), F32), jax.ShapeDtypeStruct((t, d), BF16)],
        compiler_params=_cparams("parallel"),
        name="combine_ln",
    )(*([yg] * TOP_K), gw, x, ln_g, ln_b)


def kernel(x, w_in, w_in_vres, shift_mu, shift_mu_vres, rwkv_w0, rwkv_w2, rwkv_a0, rwkv_a2, rwkv_v0, rwkv_v2, rwkv_g2, rwkv_k_k, rwkv_k_a, rwkv_r_k, rwkv_gn_g, rwkv_gn_b, lru_conv_w, lru_conv_b, lru_wa, lru_ba, lru_wx, lru_bx, lru_lambda, conf_conv_w, conf_conv_b, conf_ln_g, conf_ln_b, w_branch, w_out, ln1_g, ln1_b, router_w, router_b, exp_w_gu, exp_b_gu, exp_w_down, exp_b_down, ln2_g, ln2_b):
    batch, seq, d = x.shape
    depth = w_in.shape[0]
    c = rwkv_w0.shape[-1]
    n_rwkv = shift_mu.shape[-1]
    n_experts = router_w.shape[-1]
    t = batch * seq
    alpha = float((2 * depth) ** 0.25)
    vec = lambda a: a.reshape(1, -1)
    pad_rows = lambda w, lo, n: jnp.pad(w, ((lo, n - lo - w.shape[0]), (0, 0)))

    xf = x.reshape(t, d)
    xb = xf.astype(BF16)
    v_first = None
    for l in range(depth):
        wl = w_in[l].astype(BF16)
        d_lora = rwkv_w2.shape[1]
        d_vres = w_in_vres.shape[-1]
        w_rwkv = jnp.pad(wl[:, :n_rwkv], ((0, 0), (0, _C_VRES - n_rwkv)))
        mu = jnp.pad(shift_mu[l], (0, _C_VRES - n_rwkv))
        if l == 0:
            w_rwkv = jnp.pad(w_rwkv, ((0, 0), (0, RWKV_COLS - _C_VRES)))
            mu = jnp.pad(mu, (0, RWKV_COLS - _C_VRES))
        else:
            w_rwkv = jnp.concatenate(
                [w_rwkv, jnp.pad(w_in_vres[l - 1].astype(BF16), ((0, 0), (0, LANE - d_vres)))], axis=1)
            mu = jnp.concatenate([mu, jnp.pad(shift_mu_vres[l - 1], (0, LANE - d_vres))])
        o = n_rwkv
        p_rwkv = _matmul(xb, w_rwkv, F32, 1024, RWKV_COLS // 4)
        p_lru = _matmul(xb, wl[:, o:o + 2 * c], F32, 1024, 1024)
        p_conf = _matmul(xb, wl[:, o + 2 * c:o + 4 * c], F32, 1024, 1024)
        gates = _matmul(xb, wl[:, o + 4 * c:], BF16, 1024, 1024, act="sigmoid")

        w2 = pad_rows(rwkv_w2[l], 0, LANE)
        a2 = pad_rows(rwkv_a2[l], d_lora, LANE)
        g2 = pad_rows(rwkv_g2[l], 0, 2 * LANE)
        vres = None
        if l > 0:
            vres = (vec(rwkv_v0[l - 1]), pad_rows(rwkv_v2[l - 1], 0, LANE), v_first)
        v_t, *chunk_ops = _rwkv_prep(
            p_rwkv, vec(mu), vec(rwkv_w0[l]), vec(rwkv_a0[l]), w2, a2, g2,
            vec(rwkv_k_k[l]), vec(rwkv_k_a[l]), vec(rwkv_r_k[l]), vres, batch, seq)
        if l == 0:
            v_first = v_t
        y_a = _rwkv_chunk(chunk_ops, rwkv_gn_g[l], rwkv_gn_b[l])

        y_b = _lru(p_lru, lru_conv_w[l], vec(lru_conv_b[l]), lru_wa[l], vec(lru_ba[l]),
                   lru_wx[l], vec(lru_bx[l]), vec(lru_lambda[l]), batch, seq)
        y_c = _conformer(p_conf, conf_conv_w[l], vec(conf_conv_b[l]), vec(conf_ln_g[l]),
                         vec(conf_ln_b[l]), batch, seq)

        mixed = _merge(y_a, y_b, y_c, w_branch[l].astype(BF16), gates)
        x1, x1b, logits = _out_ln(mixed, xf, w_out[l].astype(BF16), vec(ln1_g[l]), vec(ln1_b[l]),
                                  router_w[l], vec(router_b[l]), alpha)

        gw, row_tok, dest, block_e, n_used = _route(logits, n_experts)
        xs = jnp.take(x1b, row_tok, axis=0)
        ys = _experts(xs, block_e, n_used, exp_w_gu, exp_b_gu, exp_w_down, exp_b_down, l)
        yg = ys.at[dest.T.reshape(-1)].get(mode="promise_in_bounds").reshape(TOP_K, t, d)
        xf, xb = _combine_ln(yg, gw, x1, vec(ln2_g[l]), vec(ln2_b[l]), alpha)
    return xf.reshape(batch, seq, d)
```

```python
import functools

import jax
import jax.numpy as jnp
from jax import lax
from jax.experimental import pallas as pl
from jax.experimental.pallas import tpu as pltpu

F32 = jnp.float32
BF16 = jnp.bfloat16

HEAD_DIM = 64
RWKV_CHUNK = 64
RWKV_GN_EPS = 64e-5
LRU_C = 8.0
LN_EPS = 1e-5
TOP_K = 4
MOE_BLOCK = 512
SWIGLU_ALPHA = 1.702
SWIGLU_LIMIT = 7.0
LANE = 128
SUBLANES = 8
VMEM_LIMIT = 56 * 1024 * 1024


def _cparams(*sem):
    return pltpu.CompilerParams(dimension_semantics=sem, vmem_limit_bytes=VMEM_LIMIT)


def _dot(a, b):
    return jnp.dot(a, b, preferred_element_type=F32)


def _dot_nt(a, b):
    return lax.dot_general(a, b, (((1,), (1,)), ((), ())), preferred_element_type=F32)


def _dot_tn(a, b):
    return lax.dot_general(a, b, (((0,), (0,)), ((), ())), preferred_element_type=F32)


def _split_hi_lo(w):
    hi = w.astype(BF16)
    lo = (w - hi.astype(F32)).astype(BF16)
    return hi, lo


def _dot3(a, b_hi, b_lo):
    a_hi, a_lo = _split_hi_lo(a)
    return _dot(a_hi, b_hi) + (_dot(a_lo, b_hi) + _dot(a_hi, b_lo))


def _softplus(z):
    return jnp.maximum(z, 0.0) + jnp.log(1.0 + jnp.exp(-jnp.abs(z)))


def _layer_norm(v, g, b):
    mu = jnp.mean(v, -1, keepdims=True)
    c = v - mu
    var = jnp.mean(c * c, -1, keepdims=True)
    return c * lax.rsqrt(var + LN_EPS) * g + b


def _mm_kernel(x_ref, w_ref, o_ref, *, act):
    acc = _dot(x_ref[...], w_ref[...])
    if act == "sigmoid":
        acc = jax.nn.sigmoid(acc)
    o_ref[...] = acc.astype(o_ref.dtype)


def _matmul(x, w, out_dtype, tm, tn, act=None):
    m, k = x.shape
    n = w.shape[1]
    return pl.pallas_call(
        functools.partial(_mm_kernel, act=act),
        grid=(n // tn, m // tm),
        in_specs=[pl.BlockSpec((tm, k), lambda j, i: (i, 0)),
                  pl.BlockSpec((k, tn), lambda j, i: (0, j))],
        out_specs=pl.BlockSpec((tm, tn), lambda j, i: (i, j)),
        out_shape=jax.ShapeDtypeStruct((m, n), out_dtype),
        compiler_params=_cparams("parallel", "parallel"),
        name="in_proj",
    )(x, w)


RWKV_COLS = 3584
_C_LORA = 3072
_C_GATE = 3200
_C_VRES = 3456
_N_PREP_HM = 10


def _rwkv_prep_kernel(*refs, tiles_per_batch, has_vres, c_mix):
    n_in = 18 if has_vres else 14
    (p_ref, pprev_ref, mu_ref, w0_ref, a0_ref, w2h, w2l, a2h, a2l, g2_ref,
     kkw_ref, kaw_ref, rkw_ref, seg_ref) = refs[:14]
    if has_vres:
        v0_ref, v2h, v2l, vfirst_ref = refs[14:18]
    v_o = refs[n_in]
    at_o, bt_o, kt_o, rt_o, bp_o, kp_o, vh_o, ecl_o, bonus_o, g_o = refs[n_in + 1:]
    i = pl.program_id(0)
    tm = p_ref.shape[0]
    first = (i % tiles_per_batch) == 0

    def shifted(lo, hi):
        p = p_ref[:, lo:hi]
        prev_row = jnp.where(first, 0.0, pprev_ref[7:8, lo:hi])
        rows = lax.broadcasted_iota(jnp.int32, p.shape, 0)
        prev = jnp.where(rows == 0, prev_row, pltpu.roll(p, 1, axis=0))
        return p + (prev - p) * mu_ref[:, lo:hi]

    def put(o_ref, val):
        for h in range(c_mix // HEAD_DIM):
            o_ref[0, h] = val[:, h * HEAD_DIM:(h + 1) * HEAD_DIM].astype(o_ref.dtype)

    def head_sum(val):
        hi, lo = _split_hi_lo(val)
        seg = seg_ref[...]
        parts = []
        for j in range(c_mix // LANE):
            sl = slice(j * LANE, (j + 1) * LANE)
            parts.append(_dot(hi[:, sl], seg) + _dot(lo[:, sl], seg))
        return jnp.concatenate(parts, axis=1)

    r = shifted(0, c_mix)
    k_raw = shifted(c_mix, 2 * c_mix)
    v = shifted(2 * c_mix, 3 * c_mix)
    lora = shifted(_C_LORA, _C_LORA + LANE)
    w_log = -_softplus(-(w0_ref[...] + _dot3(jnp.tanh(lora), w2h[...], w2l[...]))) - 0.5
    lw = -jnp.exp(w_log)
    alr = jax.nn.sigmoid(a0_ref[...] + _dot3(lora, a2h[...], a2l[...]))
    gate_in = jax.nn.sigmoid(shifted(_C_GATE, _C_GATE + 2 * LANE))
    put(g_o, _dot(gate_in.astype(BF16), g2_ref[...]))
    if has_vres:
        pv = shifted(_C_VRES, _C_VRES + LANE)
        mix = jax.nn.sigmoid(v0_ref[...] + _dot3(pv, v2h[...], v2l[...]))
        v = v + (vfirst_ref[...] - v) * mix
    v_o[...] = v
    put(vh_o, v)

    kk = k_raw * kkw_ref[...]
    kk = kk / jnp.maximum(jnp.sqrt(jnp.maximum(head_sum(kk * kk), 0.0)), 1e-12)
    k = k_raw * (1.0 + (alr - 1.0) * kaw_ref[...])
    put(bonus_o, head_sum(r * k * rkw_ref[...]) * v)

    rows = lax.broadcasted_iota(jnp.int32, (tm, tm), 0)
    cols = lax.broadcasted_iota(jnp.int32, (tm, tm), 1)
    same = (rows // RWKV_CHUNK) == (cols // RWKV_CHUNK)
    tril = jnp.where(same & (rows >= cols), 1.0, 0.0).astype(BF16)
    ones = jnp.where(same, 1.0, 0.0).astype(BF16)
    l0 = lw.astype(BF16)
    rem = lw - l0.astype(F32)
    l1 = rem.astype(BF16)
    l2 = (rem - l1.astype(F32)).astype(BF16)
    cum = _dot(tril, l0) + (_dot(tril, l1) + _dot(tril, l2))
    tot = _dot(ones, l0) + (_dot(ones, l1) + _dot(ones, l2))

    e_neg = jnp.exp(-cum)
    e_tail = jnp.exp(tot - cum)
    b = kk * alr
    put(at_o, -kk * jnp.exp(cum - lw))
    put(bt_o, b * e_neg)
    put(kt_o, k * e_neg)
    put(rt_o, r * jnp.exp(cum))
    put(bp_o, b * e_tail)
    put(kp_o, k * e_tail)
    put(ecl_o, jnp.exp(tot))


def _rwkv_prep(p, mu, w0, a0, w2, a2, g2, k_k, k_a, r_k, vres, batch, seq, tm=256):
    t = p.shape[0]
    c_mix = w0.shape[-1]
    n_heads = c_mix // HEAD_DIM
    has_vres = vres is not None
    tiles_per_batch = seq // tm
    row = lambda i: (i, 0)
    const = lambda i: (0, 0)
    hm = lambda i: (i // tiles_per_batch, 0, i % tiles_per_batch, 0)
    vec = pl.BlockSpec((1, c_mix), const)
    lora_w = pl.BlockSpec((LANE, c_mix), const)
    lane_id = jnp.arange(LANE) // HEAD_DIM
    seg = (lane_id[:, None] == lane_id[None, :]).astype(BF16)
    in_specs = [
        pl.BlockSpec((tm, RWKV_COLS), row),
        pl.BlockSpec((8, RWKV_COLS), lambda i: (jnp.maximum(i * (tm // 8) - 1, 0), 0)),
        pl.BlockSpec((1, RWKV_COLS), const),
        vec, vec, lora_w, lora_w, lora_w, lora_w,
        pl.BlockSpec((2 * LANE, c_mix), const),
        vec, vec, vec, pl.BlockSpec((LANE, LANE), const),
    ]
    args = [p, p, mu, w0, a0, *_split_hi_lo(w2), *_split_hi_lo(a2), g2.astype(BF16), k_k, k_a, r_k, seg]
    if has_vres:
        v0, v2, v_first = vres
        in_specs += [vec, lora_w, lora_w, pl.BlockSpec((tm, c_mix), row)]
        args += [v0, *_split_hi_lo(v2), v_first]
    hm_spec = pl.BlockSpec((1, n_heads, tm, HEAD_DIM), hm)
    hm_shape = lambda dt: jax.ShapeDtypeStruct((batch, n_heads, seq, HEAD_DIM), dt)
    out_shape = [jax.ShapeDtypeStruct((t, c_mix), F32)] + [hm_shape(BF16)] * 7 + [hm_shape(F32)] * 3
    out_specs = [pl.BlockSpec((tm, c_mix), row)] + [hm_spec] * _N_PREP_HM
    return pl.pallas_call(
        functools.partial(_rwkv_prep_kernel, tiles_per_batch=tiles_per_batch,
                          has_vres=has_vres, c_mix=c_mix),
        grid=(t // tm,), in_specs=in_specs, out_specs=out_specs, out_shape=out_shape,
        compiler_params=_cparams("parallel"), name="rwkv_prep",
    )(*args)


def _rwkv_chunk_kernel(at_ref, bt_ref, kt_ref, rt_ref, bp_ref, kp_ref, v_ref, ecl_ref, bonus_ref,
                       g_ref, gng_ref, gnb_ref, y_ref, st_ref, *, heads):
    c = pl.program_id(2)
    L = RWKV_CHUNK
    n = HEAD_DIM

    @pl.when(c == 0)
    def _():
        st_ref[...] = jnp.zeros_like(st_ref)

    ti = lax.broadcasted_iota(jnp.int32, (L, L), 0)
    si = lax.broadcasted_iota(jnp.int32, (L, L), 1)
    strict = ti > si
    incl = ti >= si
    eye = ti == si
    eye_f = jnp.where(eye, 1.0, 0.0)
    bf = lambda t: t.astype(BF16)

    hs = range(heads)
    each = lambda f, *cols: [f(*xs) for xs in zip(*cols)]
    at = [at_ref[0, h] for h in hs]
    bt = [bt_ref[0, h] for h in hs]
    kt = [kt_ref[0, h] for h in hs]
    rt = [rt_ref[0, h] for h in hs]
    vb = [v_ref[0, h] for h in hs]
    bp = [bp_ref[0, h] for h in hs]
    kp = [kp_ref[0, h] for h in hs]
    ar = each(lambda a, r: jnp.concatenate([a, r], axis=0), at, rt)
    xb = each(_dot_nt, ar, bt)
    xk = each(_dot_nt, ar, kt)
    a_ab = [jnp.where(strict, x[:L], 0.0) for x in xb]
    m_rb = [bf(jnp.where(incl, x[L:], 0.0)) for x in xb]
    r2 = lax.broadcasted_iota(jnp.int32, (2 * L, L), 0)
    s2 = lax.broadcasted_iota(jnp.int32, (2 * L, L), 1)
    strict_incl = s2 <= jnp.where(r2 < L, r2 - 1, r2 - L)
    akrk = [bf(jnp.where(strict_incl, x, 0.0)) for x in xk]
    akv = each(_dot, akrk, vb)
    p_b = [bf(a) for a in a_ab]
    tinv = [eye_f + a for a in a_ab]
    pw = each(_dot, p_b, p_b)
    for _ in range(4):
        p_b = [bf(p) for p in pw]
        pt = each(lambda p, t: _dot(jnp.concatenate([p, bf(t)], axis=0), p), p_b, tinv)
        pw = [x[:L] for x in pt]
        tinv = each(lambda t, x: t + x[L:], tinv, pt)
    tinv = each(lambda t, p: t + _dot(bf(t), bf(p)), tinv, pw)
    tb = [bf(t) for t in tinv]
    e1 = each(lambda t, x: bf(_dot(t, bf(x[:L]))), tb, akv)
    ah = each(lambda t, a: bf(_dot(t, a)), tb, at)
    rh = each(lambda r, m, a: bf(r.astype(F32) + _dot(m, a)), rt, m_rb, ah)
    o1 = each(lambda mb, e, x: _dot(mb, e) + x[L:], m_rb, e1, akv)
    phi = [bf(jnp.where(eye, ecl_ref[0, h, 0:1, :], 0.0) + _dot_tn(bp[h], ah[h])) for h in hs]
    psi = each(lambda b, e, k, v: _dot_tn(b, e) + _dot_tn(k, v), bp, e1, kp, vb)
    stb = [bf(st_ref[h]) for h in hs]
    ys = each(lambda r, p, s: _dot(jnp.concatenate([r, p], axis=0), s), rh, phi, stb)
    for h in hs:
        st_ref[h] = ys[h][L:] + psi[h]
    outs = []
    for h in hs:
        y = ys[h][:L] + o1[h]
        mu = jnp.mean(y, -1, keepdims=True)
        yc = y - mu
        var = jnp.mean(yc * yc, -1, keepdims=True)
        yn = yc * lax.rsqrt(var + RWKV_GN_EPS) * gng_ref[h] + gnb_ref[h]
        outs.append((yn + bonus_ref[0, h]) * g_ref[0, h])
    y_ref[...] = jnp.concatenate(outs, axis=1).astype(y_ref.dtype)


def _rwkv_chunk(ops, gn_g, gn_b, heads_per_step=16):
    b, h, s, n = ops[0].shape
    hb = heads_per_step
    n_chunks = s // RWKV_CHUNK
    seq_spec = pl.BlockSpec((1, hb, RWKV_CHUNK, n), lambda bi, hi, ci: (bi, hi, ci, 0))
    par_spec = pl.BlockSpec((hb, 1, n), lambda bi, hi, ci: (hi, 0, 0))
    par = lambda w: w.reshape(h, 1, n)
    return pl.pallas_call(
        functools.partial(_rwkv_chunk_kernel, heads=hb),
        grid=(b, h // hb, n_chunks),
        in_specs=[seq_spec] * _N_PREP_HM + [par_spec] * 2,
        out_specs=pl.BlockSpec((RWKV_CHUNK, hb * n), lambda bi, hi, ci: (bi * n_chunks + ci, hi)),
        out_shape=jax.ShapeDtypeStruct((b * s, h * n), BF16),
        scratch_shapes=[pltpu.VMEM((hb, n, n), F32)],
        compiler_params=_cparams("parallel", "parallel", "arbitrary"),
        name="rwkv_chunk",
    )(*ops, par(gn_g), par(gn_b))


def _lru_kernel(gate_ref, x_ref, cw_ref, cb_ref, wa_ref, ba_ref, wx_ref, bx_ref, lam_ref,
                y_ref, ext_ref, h_ref, *, conv_w):
    ti = pl.program_id(1)
    tm = x_ref.shape[0]

    @pl.when(ti == 0)
    def _():
        ext_ref[0:8, :] = jnp.zeros((8, ext_ref.shape[1]), F32)
        h_ref[...] = jnp.zeros_like(h_ref)

    x = x_ref[...]
    ext_ref[8:8 + tm, :] = x
    xc = cb_ref[...] + cw_ref[conv_w - 1:conv_w, :] * x
    for d in range(1, conv_w):
        xc = xc + cw_ref[conv_w - 1 - d:conv_w - d, :] * ext_ref[8 - d:8 - d + tm, :]
    ext_ref[0:8, :] = x[tm - 8:tm, :]
    xb = xc.astype(BF16)

    def gate(w_ref, b_ref):
        parts = [_dot(xb[:, j * LANE:(j + 1) * LANE], w_ref[j]) for j in range(w_ref.shape[0])]
        return jax.nn.sigmoid(jnp.concatenate(parts, axis=1) + b_ref[...])

    rg = gate(wa_ref, ba_ref)
    ig = gate(wx_ref, bx_ref)
    log_a = -LRU_C * rg * _softplus(-lam_ref[...])
    a = jnp.exp(log_a)
    th = jnp.tanh(log_a)
    b = jnp.sqrt(-2.0 * th / (1.0 - th)) * (ig * xc)
    in_group = lax.broadcasted_iota(jnp.int32, a.shape, 0) % SUBLANES
    d = 1
    while d < SUBLANES:
        keep = in_group >= d
        b = b + a * jnp.where(keep, pltpu.roll(b, d, axis=0), 0.0)
        a = a * jnp.where(keep, pltpu.roll(a, d, axis=0), 1.0)
        d *= 2
    carry = h_ref[...]
    groups = []
    for g in range(tm // SUBLANES):
        rows = slice(g * SUBLANES, (g + 1) * SUBLANES)
        hg = a[rows] * carry + b[rows]
        carry = hg[SUBLANES - 1:SUBLANES, :]
        groups.append(hg)
    h_ref[...] = carry
    h = jnp.concatenate(groups, axis=0)
    y_ref[...] = (h * jax.nn.gelu(gate_ref[...], approximate=True)).astype(y_ref.dtype)


def _pair_blocks(w):
    h, n, _ = w.shape
    z = jnp.zeros((h // 2, n, n), w.dtype)
    top = jnp.concatenate([w[0::2], z], axis=2)
    bot = jnp.concatenate([z, w[1::2]], axis=2)
    return jnp.concatenate([top, bot], axis=1).astype(BF16)


def _lru(proj, conv_w, conv_b, wa, ba, wx, bx, lam, batch, seq, tm=256):
    c = conv_b.shape[-1]
    tiles = seq // tm
    row0 = lambda b, i: (b * tiles + i, 0)
    row1 = lambda b, i: (b * tiles + i, 1)
    const = lambda b, i: (0, 0)
    vec = pl.BlockSpec((1, c), const)
    width = conv_w.shape[0]
    blk = pl.BlockSpec((c // LANE, LANE, LANE), lambda b, i: (0, 0, 0))
    return pl.pallas_call(
        functools.partial(_lru_kernel, conv_w=width),
        grid=(batch, tiles),
        in_specs=[pl.BlockSpec((tm, c), row0), pl.BlockSpec((tm, c), row1),
                  pl.BlockSpec((width, c), const), vec, blk, vec, blk, vec, vec],
        out_specs=pl.BlockSpec((tm, c), row0),
        out_shape=jax.ShapeDtypeStruct((batch * seq, c), BF16),
        scratch_shapes=[pltpu.VMEM((tm + 8, c), F32), pltpu.VMEM((1, c), F32)],
        compiler_params=_cparams("parallel", "arbitrary"),
        name="rglru",
    )(proj, proj, conv_w, conv_b, _pair_blocks(wa), ba, _pair_blocks(wx), bx, lam)


_CONF_HALO = 32


def _conf_kernel(val_ref, gate_ref, cw_ref, cb_ref, g_ref, b_ref, y_ref, ext_ref, sh_ref, *, conv_w):
    ti = pl.program_id(1)
    tm = val_ref.shape[0]

    @pl.when(ti == 0)
    def _():
        ext_ref[0:_CONF_HALO, :] = jnp.zeros((_CONF_HALO, ext_ref.shape[1]), F32)

    u = val_ref[...] * jax.nn.sigmoid(gate_ref[...])
    ext_ref[_CONF_HALO:_CONF_HALO + tm, :] = u
    ext = ext_ref[...]
    for r in range(1, SUBLANES):
        sh_ref[r - 1] = pltpu.roll(ext, r, axis=0)
    acc = cb_ref[...] + cw_ref[conv_w - 1:conv_w, :] * u
    for d in range(1, conv_w):
        q, r = divmod(d, SUBLANES)
        lo = _CONF_HALO - q * SUBLANES
        tap = ext_ref[lo:lo + tm, :] if r == 0 else sh_ref[r - 1, lo:lo + tm, :]
        acc = acc + cw_ref[conv_w - 1 - d:conv_w - d, :] * tap
    ext_ref[0:_CONF_HALO, :] = u[tm - _CONF_HALO:tm, :]
    y = _layer_norm(acc, g_ref[...], b_ref[...])
    y_ref[...] = (y * jax.nn.sigmoid(y)).astype(y_ref.dtype)


def _conformer(proj, conv_w, conv_b, ln_g, ln_b, batch, seq, tm=256):
    c = conv_b.shape[-1]
    tiles = seq // tm
    row0 = lambda b, i: (b * tiles + i, 0)
    row1 = lambda b, i: (b * tiles + i, 1)
    const = lambda b, i: (0, 0)
    vec = pl.BlockSpec((1, c), const)
    width = conv_w.shape[0]
    return pl.pallas_call(
        functools.partial(_conf_kernel, conv_w=width),
        grid=(batch, tiles),
        in_specs=[pl.BlockSpec((tm, c), row0), pl.BlockSpec((tm, c), row1),
                  pl.BlockSpec((width, c), const), vec, vec, vec],
        out_specs=pl.BlockSpec((tm, c), row0),
        out_shape=jax.ShapeDtypeStruct((batch * seq, c), BF16),
        scratch_shapes=[pltpu.VMEM((tm + _CONF_HALO, c), F32),
                        pltpu.VMEM((SUBLANES - 1, tm + _CONF_HALO, c), F32)],
        compiler_params=_cparams("parallel", "arbitrary"),
        name="conformer",
    )(proj, proj, conv_w, conv_b, ln_g, ln_b)


def _merge_kernel(ya_ref, yb_ref, yc_ref, wb_ref, ga_ref, gb_ref, gc_ref, o_ref):
    acc = ga_ref[...].astype(F32) * _dot(ya_ref[...], wb_ref[0])
    acc = acc + gb_ref[...].astype(F32) * _dot(yb_ref[...], wb_ref[1])
    acc = acc + gc_ref[...].astype(F32) * _dot(yc_ref[...], wb_ref[2])
    o_ref[...] = acc.astype(o_ref.dtype)


def _merge(ya, yb, yc, w_branch, gates, tm=1024, tn=1024):
    t, c = ya.shape
    d = w_branch.shape[-1]
    nj = d // tn
    ysp = pl.BlockSpec((tm, c), lambda j, i: (i, 0))
    gsp = lambda k: pl.BlockSpec((tm, tn), lambda j, i: (i, k * nj + j))
    return pl.pallas_call(
        _merge_kernel,
        grid=(nj, t // tm),
        in_specs=[ysp, ysp, ysp, pl.BlockSpec((3, c, tn), lambda j, i: (0, 0, j)),
                  gsp(0), gsp(1), gsp(2)],
        out_specs=pl.BlockSpec((tm, tn), lambda j, i: (i, j)),
        out_shape=jax.ShapeDtypeStruct((t, d), BF16),
        compiler_params=_cparams("parallel", "parallel"),
        name="merge",
    )(ya, yb, yc, w_branch, gates, gates, gates)


def _out_ln_kernel(m_ref, x_ref, w_ref, g_ref, b_ref, rwh_ref, rwl_ref, rb_ref,
                   o_ref, ob_ref, lg_ref, *, alpha):
    y = alpha * x_ref[...] + _dot(m_ref[...], w_ref[...])
    y = _layer_norm(y, g_ref[...], b_ref[...])
    o_ref[...] = y
    ob_ref[...] = y.astype(BF16)
    lg_ref[...] = _dot3(y, rwh_ref[...], rwl_ref[...]) + rb_ref[...]


def _out_ln(mixed, x, w_out, ln_g, ln_b, router_w, router_b, alpha, tm=512):
    t, d = x.shape
    row = lambda i: (i, 0)
    const = lambda i: (0, 0)
    ne = router_w.shape[-1]
    rw = jnp.pad(router_w, ((0, 0), (0, LANE - ne)))
    rb = jnp.pad(router_b, ((0, 0), (0, LANE - ne)), constant_values=-1e30)
    return pl.pallas_call(
        functools.partial(_out_ln_kernel, alpha=alpha),
        grid=(t // tm,),
        in_specs=[pl.BlockSpec((tm, d), row), pl.BlockSpec((tm, d), row), pl.BlockSpec((d, d), const),
                  pl.BlockSpec((1, d), const), pl.BlockSpec((1, d), const),
                  pl.BlockSpec((d, LANE), const), pl.BlockSpec((d, LANE), const),
                  pl.BlockSpec((1, LANE), const)],
        out_specs=[pl.BlockSpec((tm, d), row), pl.BlockSpec((tm, d), row), pl.BlockSpec((tm, LANE), row)],
        out_shape=[jax.ShapeDtypeStruct((t, d), F32), jax.ShapeDtypeStruct((t, d), BF16),
                   jax.ShapeDtypeStruct((t, LANE), F32)],
        compiler_params=_cparams("parallel"),
        name="out_ln",
    )(mixed, x, w_out, ln_g, ln_b, *_split_hi_lo(rw), rb)


def _route_kernel(lg_ref, idx_o, rank_o, gate_o, cnt_o, base_ref):
    i = pl.program_id(0)
    tm = lg_ref.shape[0]

    @pl.when(i == 0)
    def _():
        base_ref[...] = jnp.zeros_like(base_ref)

    l = lg_ref[...]
    lane = lax.broadcasted_iota(jnp.int32, l.shape, 1)
    sels, vals, idxs = [], [], []
    for _ in range(TOP_K):
        m = jnp.max(l, -1, keepdims=True)
        idx = jnp.min(jnp.where(l == m, lane, LANE), -1, keepdims=True)
        sel = lane == idx
        l = jnp.where(sel, -jnp.inf, l)
        sels.append(sel)
        vals.append(m)
        idxs.append(idx)
    ex = [jnp.exp(v - vals[0]) for v in vals]
    den = ex[0] + ex[1] + ex[2] + ex[3]
    onehot = jnp.zeros(l.shape, F32)
    for sel in sels:
        onehot = onehot + jnp.where(sel, 1.0, 0.0)
    rows = lax.broadcasted_iota(jnp.int32, (tm, tm), 0)
    cols = lax.broadcasted_iota(jnp.int32, (tm, tm), 1)
    before = jnp.where(rows > cols, 1.0, 0.0).astype(BF16)
    seen = _dot(before, onehot.astype(BF16)) + base_ref[...]
    idx_out = jnp.zeros(l.shape, jnp.int32)
    rank_out = jnp.zeros(l.shape, jnp.int32)
    gate_out = jnp.zeros(l.shape, F32)
    for k in range(TOP_K):
        rank = jnp.sum(jnp.where(sels[k], seen, 0.0), -1, keepdims=True).astype(jnp.int32)
        idx_out = jnp.where(lane == k, idxs[k], idx_out)
        rank_out = jnp.where(lane == k, rank, rank_out)
        gate_out = jnp.where(lane == k, ex[k] / den, gate_out)
    idx_o[...] = idx_out
    rank_o[...] = rank_out
    gate_o[...] = gate_out
    base_ref[...] = base_ref[...] + jnp.sum(onehot, 0, keepdims=True)
    cnt_o[...] = base_ref[...]


def _route(logits, n_experts, tm=512):
    t = logits.shape[0]
    row = lambda i: (i, 0)
    tile = pl.BlockSpec((tm, LANE), row)
    idx, rank, gates, counts = pl.pallas_call(
        _route_kernel,
        grid=(t // tm,),
        in_specs=[tile],
        out_specs=[tile, tile, tile, pl.BlockSpec((1, LANE), lambda i: (0, 0))],
        out_shape=[jax.ShapeDtypeStruct((t, LANE), jnp.int32), jax.ShapeDtypeStruct((t, LANE), jnp.int32),
                   jax.ShapeDtypeStruct((t, LANE), F32), jax.ShapeDtypeStruct((1, LANE), F32)],
        scratch_shapes=[pltpu.VMEM((1, LANE), F32)],
        compiler_params=_cparams("arbitrary"),
        name="route",
    )(logits)
    idx, rank, gates = idx[:, :TOP_K], rank[:, :TOP_K], gates[:, :TOP_K]
    counts = counts[0, :n_experts].astype(jnp.int32)
    n_blocks = -(-(t * TOP_K) // MOE_BLOCK) + n_experts
    padded = (counts + MOE_BLOCK - 1) // MOE_BLOCK * MOE_BLOCK
    pad_end = jnp.cumsum(padded)
    pad_start = pad_end - padded
    start_of = jnp.sum(jnp.where(idx[..., None] == jnp.arange(n_experts), pad_start, 0), -1)
    dest = (start_of + rank).astype(jnp.int32)
    tok = jnp.broadcast_to(jnp.arange(t, dtype=jnp.int32)[:, None], dest.shape)
    n_rows = n_blocks * MOE_BLOCK
    row_tok = (jnp.arange(n_rows, dtype=jnp.int32) % t).at[dest.reshape(-1)].set(tok.reshape(-1))
    block_start = jnp.arange(n_blocks, dtype=jnp.int32) * MOE_BLOCK
    block_e = jnp.minimum(jnp.sum(pad_end[None, :] <= block_start[:, None], -1), n_experts - 1)
    n_used = (pad_end[-1:] // MOE_BLOCK).astype(jnp.int32)
    return gates, row_tok, dest, block_e.astype(jnp.int32), n_used


_W_SLAB = 512


def _expert_kernel(be_ref, nused_ref, x_ref, *refs, d_ff, n_gu, n_dn):
    wgu = refs[:n_gu]
    bgu_ref = refs[n_gu]
    wdn = refs[n_gu + 1:n_gu + 1 + n_dn]
    bd_ref = refs[n_gu + 1 + n_dn]
    o_ref = refs[n_gu + 2 + n_dn]
    i = pl.program_id(0)

    @pl.when(i < nused_ref[0])
    def _():
        gu = bgu_ref[...]
        for q, w in enumerate(wgu):
            gu = gu + _dot(x_ref[:, q * _W_SLAB:(q + 1) * _W_SLAB], w[...].astype(BF16))
        gate = jnp.minimum(gu[:, :d_ff], SWIGLU_LIMIT)
        up = jnp.clip(gu[:, d_ff:], -SWIGLU_LIMIT, SWIGLU_LIMIT)
        h = ((up + 1.0) * gate * jax.nn.sigmoid(SWIGLU_ALPHA * gate)).astype(BF16)
        out = bd_ref[...]
        for q, w in enumerate(wdn):
            out = out + _dot(h[:, q * _W_SLAB:(q + 1) * _W_SLAB], w[...].astype(BF16))
        o_ref[...] = out.astype(o_ref.dtype)

    @pl.when(i >= nused_ref[0])
    def _():
        o_ref[...] = jnp.zeros_like(o_ref)


def _experts(xs, block_e, n_used, w_gu, b_gu, w_down, b_down, layer):
    rows, d = xs.shape
    n_blocks = rows // MOE_BLOCK
    d_ff = w_down.shape[2]
    n_gu = d // _W_SLAB
    n_dn = d_ff // _W_SLAB
    xrow = lambda i, be, nu: (jnp.minimum(i, nu[0] - 1), 0)
    slab = lambda n, q, bufs: pl.BlockSpec((None, None, _W_SLAB, n), lambda i, be, nu: (layer, be[i], q, 0),
                                           pipeline_mode=pl.Buffered(bufs))
    bias = lambda n: pl.BlockSpec((None, None, 1, n), lambda i, be, nu: (layer, be[i], 0, 0))
    grid_spec = pltpu.PrefetchScalarGridSpec(
        num_scalar_prefetch=2,
        grid=(n_blocks,),
        in_specs=([pl.BlockSpec((MOE_BLOCK, d), xrow)]
                  + [slab(2 * d_ff, q, 2 if q == 0 else 1) for q in range(n_gu)] + [bias(2 * d_ff)]
                  + [slab(d, q, 2) for q in range(n_dn)] + [bias(d)]),
        out_specs=pl.BlockSpec((MOE_BLOCK, d), lambda i, be, nu: (i, 0)),
    )
    return pl.pallas_call(
        functools.partial(_expert_kernel, d_ff=d_ff, n_gu=n_gu, n_dn=n_dn),
        grid_spec=grid_spec,
        out_shape=jax.ShapeDtypeStruct((rows, d), BF16),
        compiler_params=_cparams("arbitrary"),
        name="experts",
    )(block_e, n_used, xs, *([w_gu] * n_gu), b_gu[:, :, None, :], *([w_down] * n_dn), b_down[:, :, None, :])


def _combine_ln_kernel(*refs, alpha):
    yk = refs[:TOP_K]
    gw_ref, x_ref, g_ref, b_ref, o_ref, ob_ref = refs[TOP_K:]
    acc = alpha * x_ref[...]
    for k in range(TOP_K):
        acc = acc + gw_ref[:, k:k + 1] * yk[k][...].astype(F32)
    y = _layer_norm(acc, g_ref[...], b_ref[...])
    o_ref[...] = y
    ob_ref[...] = y.astype(BF16)


def _combine_ln(yg, gw, x, ln_g, ln_b, alpha, tm=512):
    t, d = x.shape
    row = lambda i: (i, 0)
    const = lambda i: (0, 0)
    choice = lambda k: pl.BlockSpec((None, tm, d), lambda i: (k, i, 0))
    return pl.pallas_call(
        functools.partial(_combine_ln_kernel, alpha=alpha),
        grid=(t // tm,),
        in_specs=[choice(k) for k in range(TOP_K)] + [
            pl.BlockSpec((tm, TOP_K), row), pl.BlockSpec((tm, d), row),
            pl.BlockSpec((1, d), const), pl.BlockSpec((1, d), const)],
        out_specs=[pl.BlockSpec((tm, d), row), pl.BlockSpec((tm, d), row)],
        out_shape=[jax.ShapeDtypeStruct((t, d), F32), jax.ShapeDtypeStruct((t, d), BF16)],
        compiler_params=_cparams("parallel"),
        name="combine_ln",
    )(*([yg] * TOP_K), gw, x, ln_g, ln_b)


def kernel(x, w_in, w_in_vres, shift_mu, shift_mu_vres, rwkv_w0, rwkv_w2, rwkv_a0, rwkv_a2, rwkv_v0, rwkv_v2, rwkv_g2, rwkv_k_k, rwkv_k_a, rwkv_r_k, rwkv_gn_g, rwkv_gn_b, lru_conv_w, lru_conv_b, lru_wa, lru_ba, lru_wx, lru_bx, lru_lambda, conf_conv_w, conf_conv_b, conf_ln_g, conf_ln_b, w_branch, w_out, ln1_g, ln1_b, router_w, router_b, exp_w_gu, exp_b_gu, exp_w_down, exp_b_down, ln2_g, ln2_b):
    batch, seq, d = x.shape
    depth = w_in.shape[0]
    c = rwkv_w0.shape[-1]
    n_rwkv = shift_mu.shape[-1]
    n_experts = router_w.shape[-1]
    t = batch * seq
    alpha = float((2 * depth) ** 0.25)
    vec = lambda a: a.reshape(1, -1)
    pad_rows = lambda w, lo, n: jnp.pad(w, ((lo, n - lo - w.shape[0]), (0, 0)))

    xf = x.reshape(t, d)
    xb = xf.astype(BF16)
    v_first = None
    for l in range(depth):
        wl = w_in[l].astype(BF16)
        d_lora = rwkv_w2.shape[1]
        d_vres = w_in_vres.shape[-1]
        w_rwkv = jnp.pad(wl[:, :n_rwkv], ((0, 0), (0, _C_VRES - n_rwkv)))
        mu = jnp.pad(shift_mu[l], (0, _C_VRES - n_rwkv))
        if l == 0:
            w_rwkv = jnp.pad(w_rwkv, ((0, 0), (0, RWKV_COLS - _C_VRES)))
            mu = jnp.pad(mu, (0, RWKV_COLS - _C_VRES))
        else:
            w_rwkv = jnp.concatenate(
                [w_rwkv, jnp.pad(w_in_vres[l - 1].astype(BF16), ((0, 0), (0, LANE - d_vres)))], axis=1)
            mu = jnp.concatenate([mu, jnp.pad(shift_mu_vres[l - 1], (0, LANE - d_vres))])
        o = n_rwkv
        p_rwkv = _matmul(xb, w_rwkv, F32, 1024, RWKV_COLS // 2)
        p_lru = _matmul(xb, wl[:, o:o + 2 * c], F32, 1024, 1024)
        p_conf = _matmul(xb, wl[:, o + 2 * c:o + 4 * c], F32, 1024, 1024)
        gates = _matmul(xb, wl[:, o + 4 * c:], BF16, 1024, 1024, act="sigmoid")

        w2 = pad_rows(rwkv_w2[l], 0, LANE)
        a2 = pad_rows(rwkv_a2[l], d_lora, LANE)
        g2 = pad_rows(rwkv_g2[l], 0, 2 * LANE)
        vres = None
        if l > 0:
            vres = (vec(rwkv_v0[l - 1]), pad_rows(rwkv_v2[l - 1], 0, LANE), v_first)
        v_t, *chunk_ops = _rwkv_prep(
            p_rwkv, vec(mu), vec(rwkv_w0[l]), vec(rwkv_a0[l]), w2, a2, g2,
            vec(rwkv_k_k[l]), vec(rwkv_k_a[l]), vec(rwkv_r_k[l]), vres, batch, seq)
        if l == 0:
            v_first = v_t
        y_a = _rwkv_chunk(chunk_ops, rwkv_gn_g[l], rwkv_gn_b[l])

        y_b = _lru(p_lru, lru_conv_w[l], vec(lru_conv_b[l]), lru_wa[l], vec(lru_ba[l]),
                   lru_wx[l], vec(lru_bx[l]), vec(lru_lambda[l]), batch, seq)
        y_c = _conformer(p_conf, conf_conv_w[l], vec(conf_conv_b[l]), vec(conf_ln_g[l]),
                         vec(conf_ln_b[l]), batch, seq)

        mixed = _merge(y_a, y_b, y_c, w_branch[l].astype(BF16), gates)
        x1, x1b, logits = _out_ln(mixed, xf, w_out[l].astype(BF16), vec(ln1_g[l]), vec(ln1_b[l]),
                                  router_w[l], vec(router_b[l]), alpha)

        gw, row_tok, dest, block_e, n_used = _route(logits, n_experts)
        xs = jnp.take(x1b, row_tok, axis=0)
        ys = _experts(xs, block_e, n_used, exp_w_gu, exp_b_gu, exp_w_down, exp_b_down, l)
        yg = ys.at[dest.T.reshape(-1)].get(mode="promise_in_bounds").reshape(TOP_K, t, d)
        xf, xb = _combine_ln(yg, gw, x1, vec(ln2_g[l]), vec(ln2_b[l]), alpha)
    return xf.reshape(batch, seq, d)
```

```python
import functools

import jax
import jax.numpy as jnp
from jax import lax
from jax.experimental import pallas as pl
from jax.experimental.pallas import tpu as pltpu

F32 = jnp.float32
BF16 = jnp.bfloat16

HEAD_DIM = 64
RWKV_CHUNK = 64
RWKV_GN_EPS = 64e-5
LRU_C = 8.0
LN_EPS = 1e-5
TOP_K = 4
MOE_BLOCK = 512
SWIGLU_ALPHA = 1.702
SWIGLU_LIMIT = 7.0
LANE = 128
SUBLANES = 8
VMEM_LIMIT = 56 * 1024 * 1024


def _cparams(*sem):
    return pltpu.CompilerParams(dimension_semantics=sem, vmem_limit_bytes=VMEM_LIMIT)


def _dot(a, b):
    return jnp.dot(a, b, preferred_element_type=F32)


def _dot_nt(a, b):
    return lax.dot_general(a, b, (((1,), (1,)), ((), ())), preferred_element_type=F32)


def _dot_tn(a, b):
    return lax.dot_general(a, b, (((0,), (0,)), ((), ())), preferred_element_type=F32)


def _split_hi_lo(w):
    hi = w.astype(BF16)
    lo = (w - hi.astype(F32)).astype(BF16)
    return hi, lo


def _dot3(a, b_hi, b_lo):
    a_hi, a_lo = _split_hi_lo(a)
    return _dot(a_hi, b_hi) + (_dot(a_lo, b_hi) + _dot(a_hi, b_lo))


def _softplus(z):
    return jnp.maximum(z, 0.0) + jnp.log(1.0 + jnp.exp(-jnp.abs(z)))


def _layer_norm(v, g, b):
    mu = jnp.mean(v, -1, keepdims=True)
    c = v - mu
    var = jnp.mean(c * c, -1, keepdims=True)
    return c * lax.rsqrt(var + LN_EPS) * g + b


def _mm_kernel(x_ref, w_ref, o_ref, *, act):
    acc = _dot(x_ref[...], w_ref[...])
    if act == "sigmoid":
        acc = jax.nn.sigmoid(acc)
    o_ref[...] = acc.astype(o_ref.dtype)


def _matmul(x, w, out_dtype, tm, tn, act=None):
    m, k = x.shape
    n = w.shape[1]
    return pl.pallas_call(
        functools.partial(_mm_kernel, act=act),
        grid=(n // tn, m // tm),
        in_specs=[pl.BlockSpec((tm, k), lambda j, i: (i, 0)),
                  pl.BlockSpec((k, tn), lambda j, i: (0, j))],
        out_specs=pl.BlockSpec((tm, tn), lambda j, i: (i, j)),
        out_shape=jax.ShapeDtypeStruct((m, n), out_dtype),
        compiler_params=_cparams("parallel", "parallel"),
        name="in_proj",
    )(x, w)


RWKV_COLS = 3584
_C_LORA = 3072
_C_GATE = 3200
_C_VRES = 3456
_N_PREP_HM = 10


def _rwkv_prep_kernel(*refs, tiles_per_batch, has_vres, c_mix):
    n_in = 18 if has_vres else 14
    (p_ref, pprev_ref, mu_ref, w0_ref, a0_ref, w2h, w2l, a2h, a2l, g2_ref,
     kkw_ref, kaw_ref, rkw_ref, seg_ref) = refs[:14]
    if has_vres:
        v0_ref, v2h, v2l, vfirst_ref = refs[14:18]
    v_o = refs[n_in]
    at_o, bt_o, kt_o, rt_o, bp_o, kp_o, vh_o, ecl_o, bonus_o, g_o = refs[n_in + 1:]
    i = pl.program_id(0)
    tm = p_ref.shape[0]
    first = (i % tiles_per_batch) == 0

    def shifted(lo, hi):
        p = p_ref[:, lo:hi]
        prev_row = jnp.where(first, 0.0, pprev_ref[7:8, lo:hi])
        rows = lax.broadcasted_iota(jnp.int32, p.shape, 0)
        prev = jnp.where(rows == 0, prev_row, pltpu.roll(p, 1, axis=0))
        return p + (prev - p) * mu_ref[:, lo:hi]

    def put(o_ref, val):
        for h in range(c_mix // HEAD_DIM):
            o_ref[0, h] = val[:, h * HEAD_DIM:(h + 1) * HEAD_DIM].astype(o_ref.dtype)

    def head_sum(val):
        hi, lo = _split_hi_lo(val)
        seg = seg_ref[...]
        parts = []
        for j in range(c_mix // LANE):
            sl = slice(j * LANE, (j + 1) * LANE)
            parts.append(_dot(hi[:, sl], seg) + _dot(lo[:, sl], seg))
        return jnp.concatenate(parts, axis=1)

    r = shifted(0, c_mix)
    k_raw = shifted(c_mix, 2 * c_mix)
    v = shifted(2 * c_mix, 3 * c_mix)
    lora = shifted(_C_LORA, _C_LORA + LANE)
    w_log = -_softplus(-(w0_ref[...] + _dot3(jnp.tanh(lora), w2h[...], w2l[...]))) - 0.5
    lw = -jnp.exp(w_log)
    alr = jax.nn.sigmoid(a0_ref[...] + _dot3(lora, a2h[...], a2l[...]))
    gate_in = jax.nn.sigmoid(shifted(_C_GATE, _C_GATE + 2 * LANE))
    put(g_o, _dot(gate_in.astype(BF16), g2_ref[...]))
    if has_vres:
        pv = shifted(_C_VRES, _C_VRES + LANE)
        mix = jax.nn.sigmoid(v0_ref[...] + _dot3(pv, v2h[...], v2l[...]))
        v = v + (vfirst_ref[...] - v) * mix
    v_o[...] = v
    put(vh_o, v)

    kk = k_raw * kkw_ref[...]
    kk = kk / jnp.maximum(jnp.sqrt(jnp.maximum(head_sum(kk * kk), 0.0)), 1e-12)
    k = k_raw * (1.0 + (alr - 1.0) * kaw_ref[...])
    put(bonus_o, head_sum(r * k * rkw_ref[...]) * v)

    rows = lax.broadcasted_iota(jnp.int32, (tm, tm), 0)
    cols = lax.broadcasted_iota(jnp.int32, (tm, tm), 1)
    same = (rows // RWKV_CHUNK) == (cols // RWKV_CHUNK)
    tril = jnp.where(same & (rows >= cols), 1.0, 0.0).astype(BF16)
    ones = jnp.where(same, 1.0, 0.0).astype(BF16)
    l0 = lw.astype(BF16)
    rem = lw - l0.astype(F32)
    l1 = rem.astype(BF16)
    l2 = (rem - l1.astype(F32)).astype(BF16)
    cum = _dot(tril, l0) + (_dot(tril, l1) + _dot(tril, l2))
    tot = _dot(ones, l0) + (_dot(ones, l1) + _dot(ones, l2))

    e_neg = jnp.exp(-cum)
    e_tail = jnp.exp(tot - cum)
    b = kk * alr
    put(at_o, -kk * jnp.exp(cum - lw))
    put(bt_o, b * e_neg)
    put(kt_o, k * e_neg)
    put(rt_o, r * jnp.exp(cum))
    put(bp_o, b * e_tail)
    put(kp_o, k * e_tail)
    put(ecl_o, jnp.exp(tot))


def _rwkv_prep(p, mu, w0, a0, w2, a2, g2, k_k, k_a, r_k, vres, batch, seq, tm=256):
    t = p.shape[0]
    c_mix = w0.shape[-1]
    n_heads = c_mix // HEAD_DIM
    has_vres = vres is not None
    tiles_per_batch = seq // tm
    row = lambda i: (i, 0)
    const = lambda i: (0, 0)
    hm = lambda i: (i // tiles_per_batch, 0, i % tiles_per_batch, 0)
    vec = pl.BlockSpec((1, c_mix), const)
    lora_w = pl.BlockSpec((LANE, c_mix), const)
    lane_id = jnp.arange(LANE) // HEAD_DIM
    seg = (lane_id[:, None] == lane_id[None, :]).astype(BF16)
    in_specs = [
        pl.BlockSpec((tm, RWKV_COLS), row),
        pl.BlockSpec((8, RWKV_COLS), lambda i: (jnp.maximum(i * (tm // 8) - 1, 0), 0)),
        pl.BlockSpec((1, RWKV_COLS), const),
        vec, vec, lora_w, lora_w, lora_w, lora_w,
        pl.BlockSpec((2 * LANE, c_mix), const),
        vec, vec, vec, pl.BlockSpec((LANE, LANE), const),
    ]
    args = [p, p, mu, w0, a0, *_split_hi_lo(w2), *_split_hi_lo(a2), g2.astype(BF16), k_k, k_a, r_k, seg]
    if has_vres:
        v0, v2, v_first = vres
        in_specs += [vec, lora_w, lora_w, pl.BlockSpec((tm, c_mix), row)]
        args += [v0, *_split_hi_lo(v2), v_first]
    hm_spec = pl.BlockSpec((1, n_heads, tm, HEAD_DIM), hm)
    hm_shape = lambda dt: jax.ShapeDtypeStruct((batch, n_heads, seq, HEAD_DIM), dt)
    out_shape = [jax.ShapeDtypeStruct((t, c_mix), F32)] + [hm_shape(BF16)] * 7 + [hm_shape(F32)] * 3
    out_specs = [pl.BlockSpec((tm, c_mix), row)] + [hm_spec] * _N_PREP_HM
    return pl.pallas_call(
        functools.partial(_rwkv_prep_kernel, tiles_per_batch=tiles_per_batch,
                          has_vres=has_vres, c_mix=c_mix),
        grid=(t // tm,), in_specs=in_specs, out_specs=out_specs, out_shape=out_shape,
        compiler_params=_cparams("parallel"), name="rwkv_prep",
    )(*args)


def _rwkv_chunk_kernel(at_ref, bt_ref, kt_ref, rt_ref, bp_ref, kp_ref, v_ref, ecl_ref, bonus_ref,
                       g_ref, gng_ref, gnb_ref, y_ref, st_ref, *, heads):
    c = pl.program_id(2)
    L = RWKV_CHUNK
    n = HEAD_DIM

    @pl.when(c == 0)
    def _():
        st_ref[...] = jnp.zeros_like(st_ref)

    ti = lax.broadcasted_iota(jnp.int32, (L, L), 0)
    si = lax.broadcasted_iota(jnp.int32, (L, L), 1)
    strict = ti > si
    incl = ti >= si
    eye = ti == si
    eye_f = jnp.where(eye, 1.0, 0.0)
    bf = lambda t: t.astype(BF16)

    hs = range(heads)
    each = lambda f, *cols: [f(*xs) for xs in zip(*cols)]
    at = [at_ref[0, h] for h in hs]
    bt = [bt_ref[0, h] for h in hs]
    kt = [kt_ref[0, h] for h in hs]
    rt = [rt_ref[0, h] for h in hs]
    vb = [v_ref[0, h] for h in hs]
    bp = [bp_ref[0, h] for h in hs]
    kp = [kp_ref[0, h] for h in hs]
    ar = each(lambda a, r: jnp.concatenate([a, r], axis=0), at, rt)
    xb = each(_dot_nt, ar, bt)
    xk = each(_dot_nt, ar, kt)
    a_ab = [jnp.where(strict, x[:L], 0.0) for x in xb]
    m_rb = [bf(jnp.where(incl, x[L:], 0.0)) for x in xb]
    r2 = lax.broadcasted_iota(jnp.int32, (2 * L, L), 0)
    s2 = lax.broadcasted_iota(jnp.int32, (2 * L, L), 1)
    strict_incl = s2 <= jnp.where(r2 < L, r2 - 1, r2 - L)
    akrk = [bf(jnp.where(strict_incl, x, 0.0)) for x in xk]
    akv = each(_dot, akrk, vb)
    p_b = [bf(a) for a in a_ab]
    tinv = [eye_f + a for a in a_ab]
    pw = each(_dot, p_b, p_b)
    for _ in range(4):
        p_b = [bf(p) for p in pw]
        pt = each(lambda p, t: _dot(jnp.concatenate([p, bf(t)], axis=0), p), p_b, tinv)
        pw = [x[:L] for x in pt]
        tinv = each(lambda t, x: t + x[L:], tinv, pt)
    tinv = each(lambda t, p: t + _dot(bf(t), bf(p)), tinv, pw)
    tb = [bf(t) for t in tinv]
    e1 = each(lambda t, x: bf(_dot(t, bf(x[:L]))), tb, akv)
    ah = each(lambda t, a: bf(_dot(t, a)), tb, at)
    rh = each(lambda r, m, a: bf(r.astype(F32) + _dot(m, a)), rt, m_rb, ah)
    o1 = each(lambda mb, e, x: _dot(mb, e) + x[L:], m_rb, e1, akv)
    phi = [bf(jnp.where(eye, ecl_ref[0, h, 0:1, :], 0.0) + _dot_tn(bp[h], ah[h])) for h in hs]
    psi = each(lambda b, e, k, v: _dot_tn(jnp.concatenate([b, k], axis=0), jnp.concatenate([e, v], axis=0)),
               bp, e1, kp, vb)
    stb = [bf(st_ref[h]) for h in hs]
    ys = each(lambda r, p, s: _dot(jnp.concatenate([r, p], axis=0), s), rh, phi, stb)
    for h in hs:
        st_ref[h] = ys[h][L:] + psi[h]
    outs = []
    for h in hs:
        y = ys[h][:L] + o1[h]
        mu = jnp.mean(y, -1, keepdims=True)
        yc = y - mu
        var = jnp.mean(yc * yc, -1, keepdims=True)
        yn = yc * lax.rsqrt(var + RWKV_GN_EPS) * gng_ref[h] + gnb_ref[h]
        outs.append((yn + bonus_ref[0, h]) * g_ref[0, h])
    y_ref[...] = jnp.concatenate(outs, axis=1).astype(y_ref.dtype)


def _rwkv_chunk(ops, gn_g, gn_b, heads_per_step=16):
    b, h, s, n = ops[0].shape
    hb = heads_per_step
    n_chunks = s // RWKV_CHUNK
    seq_spec = pl.BlockSpec((1, hb, RWKV_CHUNK, n), lambda bi, hi, ci: (bi, hi, ci, 0))
    par_spec = pl.BlockSpec((hb, 1, n), lambda bi, hi, ci: (hi, 0, 0))
    par = lambda w: w.reshape(h, 1, n)
    return pl.pallas_call(
        functools.partial(_rwkv_chunk_kernel, heads=hb),
        grid=(b, h // hb, n_chunks),
        in_specs=[seq_spec] * _N_PREP_HM + [par_spec] * 2,
        out_specs=pl.BlockSpec((RWKV_CHUNK, hb * n), lambda bi, hi, ci: (bi * n_chunks + ci, hi)),
        out_shape=jax.ShapeDtypeStruct((b * s, h * n), BF16),
        scratch_shapes=[pltpu.VMEM((hb, n, n), F32)],
        compiler_params=_cparams("parallel", "parallel", "arbitrary"),
        name="rwkv_chunk",
    )(*ops, par(gn_g), par(gn_b))


def _lru_kernel(gate_ref, x_ref, cw_ref, cb_ref, wa_ref, ba_ref, wx_ref, bx_ref, lam_ref,
                y_ref, ext_ref, h_ref, *, conv_w):
    ti = pl.program_id(1)
    tm = x_ref.shape[0]

    @pl.when(ti == 0)
    def _():
        ext_ref[0:8, :] = jnp.zeros((8, ext_ref.shape[1]), F32)
        h_ref[...] = jnp.zeros_like(h_ref)

    x = x_ref[...]
    ext_ref[8:8 + tm, :] = x
    xc = cb_ref[...] + cw_ref[conv_w - 1:conv_w, :] * x
    for d in range(1, conv_w):
        xc = xc + cw_ref[conv_w - 1 - d:conv_w - d, :] * ext_ref[8 - d:8 - d + tm, :]
    ext_ref[0:8, :] = x[tm - 8:tm, :]
    xb = xc.astype(BF16)

    def gate(w_ref, b_ref):
        parts = [_dot(xb[:, j * LANE:(j + 1) * LANE], w_ref[j]) for j in range(w_ref.shape[0])]
        return jax.nn.sigmoid(jnp.concatenate(parts, axis=1) + b_ref[...])

    rg = gate(wa_ref, ba_ref)
    ig = gate(wx_ref, bx_ref)
    log_a = -LRU_C * rg * _softplus(-lam_ref[...])
    a = jnp.exp(log_a)
    th = jnp.tanh(log_a)
    b = jnp.sqrt(-2.0 * th / (1.0 - th)) * (ig * xc)
    in_group = lax.broadcasted_iota(jnp.int32, a.shape, 0) % SUBLANES
    d = 1
    while d < SUBLANES:
        keep = in_group >= d
        b = b + a * jnp.where(keep, pltpu.roll(b, d, axis=0), 0.0)
        a = a * jnp.where(keep, pltpu.roll(a, d, axis=0), 1.0)
        d *= 2
    carry = h_ref[...]
    groups = []
    for g in range(tm // SUBLANES):
        rows = slice(g * SUBLANES, (g + 1) * SUBLANES)
        hg = a[rows] * carry + b[rows]
        carry = hg[SUBLANES - 1:SUBLANES, :]
        groups.append(hg)
    h_ref[...] = carry
    h = jnp.concatenate(groups, axis=0)
    y_ref[...] = (h * jax.nn.gelu(gate_ref[...], approximate=True)).astype(y_ref.dtype)


def _pair_blocks(w):
    h, n, _ = w.shape
    z = jnp.zeros((h // 2, n, n), w.dtype)
    top = jnp.concatenate([w[0::2], z], axis=2)
    bot = jnp.concatenate([z, w[1::2]], axis=2)
    return jnp.concatenate([top, bot], axis=1).astype(BF16)


def _lru(proj, conv_w, conv_b, wa, ba, wx, bx, lam, batch, seq, tm=256):
    c = conv_b.shape[-1]
    tiles = seq // tm
    row0 = lambda b, i: (b * tiles + i, 0)
    row1 = lambda b, i: (b * tiles + i, 1)
    const = lambda b, i: (0, 0)
    vec = pl.BlockSpec((1, c), const)
    width = conv_w.shape[0]
    blk = pl.BlockSpec((c // LANE, LANE, LANE), lambda b, i: (0, 0, 0))
    return pl.pallas_call(
        functools.partial(_lru_kernel, conv_w=width),
        grid=(batch, tiles),
        in_specs=[pl.BlockSpec((tm, c), row0), pl.BlockSpec((tm, c), row1),
                  pl.BlockSpec((width, c), const), vec, blk, vec, blk, vec, vec],
        out_specs=pl.BlockSpec((tm, c), row0),
        out_shape=jax.ShapeDtypeStruct((batch * seq, c), BF16),
        scratch_shapes=[pltpu.VMEM((tm + 8, c), F32), pltpu.VMEM((1, c), F32)],
        compiler_params=_cparams("parallel", "arbitrary"),
        name="rglru",
    )(proj, proj, conv_w, conv_b, _pair_blocks(wa), ba, _pair_blocks(wx), bx, lam)


_CONF_HALO = 32


def _conf_kernel(val_ref, gate_ref, cw_ref, cb_ref, g_ref, b_ref, y_ref, ext_ref, sh_ref, *, conv_w):
    ti = pl.program_id(1)
    tm = val_ref.shape[0]

    @pl.when(ti == 0)
    def _():
        ext_ref[0:_CONF_HALO, :] = jnp.zeros((_CONF_HALO, ext_ref.shape[1]), F32)

    u = val_ref[...] * jax.nn.sigmoid(gate_ref[...])
    ext_ref[_CONF_HALO:_CONF_HALO + tm, :] = u
    ext = ext_ref[...]
    for r in range(1, SUBLANES):
        sh_ref[r - 1] = pltpu.roll(ext, r, axis=0)
    acc = cb_ref[...] + cw_ref[conv_w - 1:conv_w, :] * u
    for d in range(1, conv_w):
        q, r = divmod(d, SUBLANES)
        lo = _CONF_HALO - q * SUBLANES
        tap = ext_ref[lo:lo + tm, :] if r == 0 else sh_ref[r - 1, lo:lo + tm, :]
        acc = acc + cw_ref[conv_w - 1 - d:conv_w - d, :] * tap
    ext_ref[0:_CONF_HALO, :] = u[tm - _CONF_HALO:tm, :]
    y = _layer_norm(acc, g_ref[...], b_ref[...])
    y_ref[...] = (y * jax.nn.sigmoid(y)).astype(y_ref.dtype)


def _conformer(proj, conv_w, conv_b, ln_g, ln_b, batch, seq, tm=256):
    c = conv_b.shape[-1]
    tiles = seq // tm
    row0 = lambda b, i: (b * tiles + i, 0)
    row1 = lambda b, i: (b * tiles + i, 1)
    const = lambda b, i: (0, 0)
    vec = pl.BlockSpec((1, c), const)
    width = conv_w.shape[0]
    return pl.pallas_call(
        functools.partial(_conf_kernel, conv_w=width),
        grid=(batch, tiles),
        in_specs=[pl.BlockSpec((tm, c), row0), pl.BlockSpec((tm, c), row1),
                  pl.BlockSpec((width, c), const), vec, vec, vec],
        out_specs=pl.BlockSpec((tm, c), row0),
        out_shape=jax.ShapeDtypeStruct((batch * seq, c), BF16),
        scratch_shapes=[pltpu.VMEM((tm + _CONF_HALO, c), F32),
                        pltpu.VMEM((SUBLANES - 1, tm + _CONF_HALO, c), F32)],
        compiler_params=_cparams("parallel", "arbitrary"),
        name="conformer",
    )(proj, proj, conv_w, conv_b, ln_g, ln_b)


def _merge_kernel(ya_ref, yb_ref, yc_ref, wb_ref, ga_ref, gb_ref, gc_ref, o_ref):
    acc = ga_ref[...].astype(F32) * _dot(ya_ref[...], wb_ref[0])
    acc = acc + gb_ref[...].astype(F32) * _dot(yb_ref[...], wb_ref[1])
    acc = acc + gc_ref[...].astype(F32) * _dot(yc_ref[...], wb_ref[2])
    o_ref[...] = acc.astype(o_ref.dtype)


def _merge(ya, yb, yc, w_branch, gates, tm=1024, tn=1024):
    t, c = ya.shape
    d = w_branch.shape[-1]
    nj = d // tn
    ysp = pl.BlockSpec((tm, c), lambda j, i: (i, 0))
    gsp = lambda k: pl.BlockSpec((tm, tn), lambda j, i: (i, k * nj + j))
    return pl.pallas_call(
        _merge_kernel,
        grid=(nj, t // tm),
        in_specs=[ysp, ysp, ysp, pl.BlockSpec((3, c, tn), lambda j, i: (0, 0, j)),
                  gsp(0), gsp(1), gsp(2)],
        out_specs=pl.BlockSpec((tm, tn), lambda j, i: (i, j)),
        out_shape=jax.ShapeDtypeStruct((t, d), BF16),
        compiler_params=_cparams("parallel", "parallel"),
        name="merge",
    )(ya, yb, yc, w_branch, gates, gates, gates)


def _out_ln_kernel(m_ref, x_ref, w_ref, g_ref, b_ref, rwh_ref, rwl_ref, rb_ref,
                   o_ref, ob_ref, lg_ref, *, alpha):
    y = alpha * x_ref[...] + _dot(m_ref[...], w_ref[...])
    y = _layer_norm(y, g_ref[...], b_ref[...])
    o_ref[...] = y
    ob_ref[...] = y.astype(BF16)
    lg_ref[...] = _dot3(y, rwh_ref[...], rwl_ref[...]) + rb_ref[...]


def _out_ln(mixed, x, w_out, ln_g, ln_b, router_w, router_b, alpha, tm=512):
    t, d = x.shape
    row = lambda i: (i, 0)
    const = lambda i: (0, 0)
    ne = router_w.shape[-1]
    rw = jnp.pad(router_w, ((0, 0), (0, LANE - ne)))
    rb = jnp.pad(router_b, ((0, 0), (0, LANE - ne)), constant_values=-1e30)
    return pl.pallas_call(
        functools.partial(_out_ln_kernel, alpha=alpha),
        grid=(t // tm,),
        in_specs=[pl.BlockSpec((tm, d), row), pl.BlockSpec((tm, d), row), pl.BlockSpec((d, d), const),
                  pl.BlockSpec((1, d), const), pl.BlockSpec((1, d), const),
                  pl.BlockSpec((d, LANE), const), pl.BlockSpec((d, LANE), const),
                  pl.BlockSpec((1, LANE), const)],
        out_specs=[pl.BlockSpec((tm, d), row), pl.BlockSpec((tm, d), row), pl.BlockSpec((tm, LANE), row)],
        out_shape=[jax.ShapeDtypeStruct((t, d), F32), jax.ShapeDtypeStruct((t, d), BF16),
                   jax.ShapeDtypeStruct((t, LANE), F32)],
        compiler_params=_cparams("parallel"),
        name="out_ln",
    )(mixed, x, w_out, ln_g, ln_b, *_split_hi_lo(rw), rb)


def _route_kernel(lg_ref, idx_o, rank_o, gate_o, cnt_o, base_ref):
    i = pl.program_id(0)
    tm = lg_ref.shape[0]

    @pl.when(i == 0)
    def _():
        base_ref[...] = jnp.zeros_like(base_ref)

    l = lg_ref[...]
    lane = lax.broadcasted_iota(jnp.int32, l.shape, 1)
    sels, vals, idxs = [], [], []
    for _ in range(TOP_K):
        m = jnp.max(l, -1, keepdims=True)
        idx = jnp.min(jnp.where(l == m, lane, LANE), -1, keepdims=True)
        sel = lane == idx
        l = jnp.where(sel, -jnp.inf, l)
        sels.append(sel)
        vals.append(m)
        idxs.append(idx)
    ex = [jnp.exp(v - vals[0]) for v in vals]
    den = ex[0] + ex[1] + ex[2] + ex[3]
    onehot = jnp.zeros(l.shape, F32)
    for sel in sels:
        onehot = onehot + jnp.where(sel, 1.0, 0.0)
    rows = lax.broadcasted_iota(jnp.int32, (tm, tm), 0)
    cols = lax.broadcasted_iota(jnp.int32, (tm, tm), 1)
    before = jnp.where(rows > cols, 1.0, 0.0).astype(BF16)
    seen = _dot(before, onehot.astype(BF16)) + base_ref[...]
    idx_out = jnp.zeros(l.shape, jnp.int32)
    rank_out = jnp.zeros(l.shape, jnp.int32)
    gate_out = jnp.zeros(l.shape, F32)
    for k in range(TOP_K):
        rank = jnp.sum(jnp.where(sels[k], seen, 0.0), -1, keepdims=True).astype(jnp.int32)
        idx_out = jnp.where(lane == k, idxs[k], idx_out)
        rank_out = jnp.where(lane == k, rank, rank_out)
        gate_out = jnp.where(lane == k, ex[k] / den, gate_out)
    idx_o[...] = idx_out
    rank_o[...] = rank_out
    gate_o[...] = gate_out
    base_ref[...] = base_ref[...] + jnp.sum(onehot, 0, keepdims=True)
    cnt_o[...] = base_ref[...]


def _route(logits, n_experts, tm=512):
    t = logits.shape[0]
    row = lambda i: (i, 0)
    tile = pl.BlockSpec((tm, LANE), row)
    idx, rank, gates, counts = pl.pallas_call(
        _route_kernel,
        grid=(t // tm,),
        in_specs=[tile],
        out_specs=[tile, tile, tile, pl.BlockSpec((1, LANE), lambda i: (0, 0))],
        out_shape=[jax.ShapeDtypeStruct((t, LANE), jnp.int32), jax.ShapeDtypeStruct((t, LANE), jnp.int32),
                   jax.ShapeDtypeStruct((t, LANE), F32), jax.ShapeDtypeStruct((1, LANE), F32)],
        scratch_shapes=[pltpu.VMEM((1, LANE), F32)],
        compiler_params=_cparams("arbitrary"),
        name="route",
    )(logits)
    idx, rank, gates = idx[:, :TOP_K], rank[:, :TOP_K], gates[:, :TOP_K]
    counts = counts[0, :n_experts].astype(jnp.int32)
    n_blocks = -(-(t * TOP_K) // MOE_BLOCK) + n_experts
    padded = (counts + MOE_BLOCK - 1) // MOE_BLOCK * MOE_BLOCK
    pad_end = jnp.cumsum(padded)
    pad_start = pad_end - padded
    start_of = jnp.sum(jnp.where(idx[..., None] == jnp.arange(n_experts), pad_start, 0), -1)
    dest = (start_of + rank).astype(jnp.int32)
    tok = jnp.broadcast_to(jnp.arange(t, dtype=jnp.int32)[:, None], dest.shape)
    n_rows = n_blocks * MOE_BLOCK
    row_tok = (jnp.arange(n_rows, dtype=jnp.int32) % t).at[dest.reshape(-1)].set(tok.reshape(-1))
    block_start = jnp.arange(n_blocks, dtype=jnp.int32) * MOE_BLOCK
    block_e = jnp.minimum(jnp.sum(pad_end[None, :] <= block_start[:, None], -1), n_experts - 1)
    n_used = (pad_end[-1:] // MOE_BLOCK).astype(jnp.int32)
    return gates, row_tok, dest, block_e.astype(jnp.int32), n_used


_W_SLAB = 512


def _expert_kernel(be_ref, nused_ref, x_ref, *refs, d_ff, n_gu, n_dn):
    wgu = refs[:n_gu]
    bgu_ref = refs[n_gu]
    wdn = refs[n_gu + 1:n_gu + 1 + n_dn]
    bd_ref = refs[n_gu + 1 + n_dn]
    o_ref = refs[n_gu + 2 + n_dn]
    i = pl.program_id(0)

    @pl.when(i < nused_ref[0])
    def _():
        gu = bgu_ref[...]
        for q, w in enumerate(wgu):
            gu = gu + _dot(x_ref[:, q * _W_SLAB:(q + 1) * _W_SLAB], w[...].astype(BF16))
        gate = jnp.minimum(gu[:, :d_ff], SWIGLU_LIMIT)
        up = jnp.clip(gu[:, d_ff:], -SWIGLU_LIMIT, SWIGLU_LIMIT)
        h = ((up + 1.0) * gate * jax.nn.sigmoid(SWIGLU_ALPHA * gate)).astype(BF16)
        out = bd_ref[...]
        for q, w in enumerate(wdn):
            out = out + _dot(h[:, q * _W_SLAB:(q + 1) * _W_SLAB], w[...].astype(BF16))
        o_ref[...] = out.astype(o_ref.dtype)

    @pl.when(i >= nused_ref[0])
    def _():
        o_ref[...] = jnp.zeros_like(o_ref)


def _experts(xs, block_e, n_used, w_gu, b_gu, w_down, b_down, layer):
    rows, d = xs.shape
    n_blocks = rows // MOE_BLOCK
    d_ff = w_down.shape[2]
    n_gu = d // _W_SLAB
    n_dn = d_ff // _W_SLAB
    xrow = lambda i, be, nu: (jnp.minimum(i, nu[0] - 1), 0)
    slab = lambda n, q, bufs: pl.BlockSpec((None, None, _W_SLAB, n), lambda i, be, nu: (layer, be[i], q, 0),
                                           pipeline_mode=pl.Buffered(bufs))
    bias = lambda n: pl.BlockSpec((None, None, 1, n), lambda i, be, nu: (layer, be[i], 0, 0))
    grid_spec = pltpu.PrefetchScalarGridSpec(
        num_scalar_prefetch=2,
        grid=(n_blocks,),
        in_specs=([pl.BlockSpec((MOE_BLOCK, d), xrow)]
                  + [slab(2 * d_ff, q, 2 if q == 0 else 1) for q in range(n_gu)] + [bias(2 * d_ff)]
                  + [slab(d, q, 2) for q in range(n_dn)] + [bias(d)]),
        out_specs=pl.BlockSpec((MOE_BLOCK, d), lambda i, be, nu: (i, 0)),
    )
    return pl.pallas_call(
        functools.partial(_expert_kernel, d_ff=d_ff, n_gu=n_gu, n_dn=n_dn),
        grid_spec=grid_spec,
        out_shape=jax.ShapeDtypeStruct((rows, d), BF16),
        compiler_params=_cparams("arbitrary"),
        name="experts",
    )(block_e, n_used, xs, *([w_gu] * n_gu), b_gu[:, :, None, :], *([w_down] * n_dn), b_down[:, :, None, :])


def _combine_ln_kernel(*refs, alpha):
    yk = refs[:TOP_K]
    gw_ref, x_ref, g_ref, b_ref, o_ref, ob_ref = refs[TOP_K:]
    acc = alpha * x_ref[...]
    for k in range(TOP_K):
        acc = acc + gw_ref[:, k:k + 1] * yk[k][...].astype(F32)
    y = _layer_norm(acc, g_ref[...], b_ref[...])
    o_ref[...] = y
    ob_ref[...] = y.astype(BF16)


def _combine_ln(yg, gw, x, ln_g, ln_b, alpha, tm=512):
    t, d = x.shape
    row = lambda i: (i, 0)
    const = lambda i: (0, 0)
    choice = lambda k: pl.BlockSpec((None, tm, d), lambda i: (k, i, 0))
    return pl.pallas_call(
        functools.partial(_combine_ln_kernel, alpha=alpha),
        grid=(t // tm,),
        in_specs=[choice(k) for k in range(TOP_K)] + [
            pl.BlockSpec((tm, TOP_K), row), pl.BlockSpec((tm, d), row),
            pl.BlockSpec((1, d), const), pl.BlockSpec((1, d), const)],
        out_specs=[pl.BlockSpec((tm, d), row), pl.BlockSpec((tm, d), row)],
        out_shape=[jax.ShapeDtypeStruct((t, d), F32), jax.ShapeDtypeStruct((t, d), BF16)],
        compiler_params=_cparams("parallel"),
        name="combine_ln",
    )(*([yg] * TOP_K), gw, x, ln_g, ln_b)


def kernel(x, w_in, w_in_vres, shift_mu, shift_mu_vres, rwkv_w0, rwkv_w2, rwkv_a0, rwkv_a2, rwkv_v0, rwkv_v2, rwkv_g2, rwkv_k_k, rwkv_k_a, rwkv_r_k, rwkv_gn_g, rwkv_gn_b, lru_conv_w, lru_conv_b, lru_wa, lru_ba, lru_wx, lru_bx, lru_lambda, conf_conv_w, conf_conv_b, conf_ln_g, conf_ln_b, w_branch, w_out, ln1_g, ln1_b, router_w, router_b, exp_w_gu, exp_b_gu, exp_w_down, exp_b_down, ln2_g, ln2_b):
    batch, seq, d = x.shape
    depth = w_in.shape[0]
    c = rwkv_w0.shape[-1]
    n_rwkv = shift_mu.shape[-1]
    n_experts = router_w.shape[-1]
    t = batch * seq
    alpha = float((2 * depth) ** 0.25)
    vec = lambda a: a.reshape(1, -1)
    pad_rows = lambda w, lo, n: jnp.pad(w, ((lo, n - lo - w.shape[0]), (0, 0)))

    xf = x.reshape(t, d)
    xb = xf.astype(BF16)
    v_first = None
    for l in range(depth):
        wl = w_in[l].astype(BF16)
        d_lora = rwkv_w2.shape[1]
        d_vres = w_in_vres.shape[-1]
        w_rwkv = jnp.pad(wl[:, :n_rwkv], ((0, 0), (0, _C_VRES - n_rwkv)))
        mu = jnp.pad(shift_mu[l], (0, _C_VRES - n_rwkv))
        if l == 0:
            w_rwkv = jnp.pad(w_rwkv, ((0, 0), (0, RWKV_COLS - _C_VRES)))
            mu = jnp.pad(mu, (0, RWKV_COLS - _C_VRES))
        else:
            w_rwkv = jnp.concatenate(
                [w_rwkv, jnp.pad(w_in_vres[l - 1].astype(BF16), ((0, 0), (0, LANE - d_vres)))], axis=1)
            mu = jnp.concatenate([mu, jnp.pad(shift_mu_vres[l - 1], (0, LANE - d_vres))])
        o = n_rwkv
        p_rwkv = _matmul(xb, w_rwkv, F32, 1024, RWKV_COLS // 2)
        p_lru = _matmul(xb, wl[:, o:o + 2 * c], F32, 1024, 2 * c)
        p_conf = _matmul(xb, wl[:, o + 2 * c:o + 4 * c], F32, 1024, 2 * c)
        gates = _matmul(xb, wl[:, o + 4 * c:], BF16, 1024, 2 * c, act="sigmoid")

        w2 = pad_rows(rwkv_w2[l], 0, LANE)
        a2 = pad_rows(rwkv_a2[l], d_lora, LANE)
        g2 = pad_rows(rwkv_g2[l], 0, 2 * LANE)
        vres = None
        if l > 0:
            vres = (vec(rwkv_v0[l - 1]), pad_rows(rwkv_v2[l - 1], 0, LANE), v_first)
        v_t, *chunk_ops = _rwkv_prep(
            p_rwkv, vec(mu), vec(rwkv_w0[l]), vec(rwkv_a0[l]), w2, a2, g2,
            vec(rwkv_k_k[l]), vec(rwkv_k_a[l]), vec(rwkv_r_k[l]), vres, batch, seq)
        if l == 0:
            v_first = v_t
        y_a = _rwkv_chunk(chunk_ops, rwkv_gn_g[l], rwkv_gn_b[l])

        y_b = _lru(p_lru, lru_conv_w[l], vec(lru_conv_b[l]), lru_wa[l], vec(lru_ba[l]),
                   lru_wx[l], vec(lru_bx[l]), vec(lru_lambda[l]), batch, seq)
        y_c = _conformer(p_conf, conf_conv_w[l], vec(conf_conv_b[l]), vec(conf_ln_g[l]),
                         vec(conf_ln_b[l]), batch, seq)

        mixed = _merge(y_a, y_b, y_c, w_branch[l].astype(BF16), gates)
        x1, x1b, logits = _out_ln(mixed, xf, w_out[l].astype(BF16), vec(ln1_g[l]), vec(ln1_b[l]),
                                  router_w[l], vec(router_b[l]), alpha)

        gw, row_tok, dest, block_e, n_used = _route(logits, n_experts)
        xs = jnp.take(x1b, row_tok, axis=0)
        ys = _experts(xs, block_e, n_used, exp_w_gu, exp_b_gu, exp_w_down, exp_b_down, l)
        yg = ys.at[dest.T.reshape(-1)].get(mode="promise_in_bounds").reshape(TOP_K, t, d)
        xf, xb = _combine_ln(yg, gw, x1, vec(ln2_g[l]), vec(ln2_b[l]), alpha)
    return xf.reshape(batch, seq, d)
```
